```python
import jax, jax.numpy as jnp
from jax import lax
from jax.lax import linalg as lax_linalg
import numpy as np

D_MODEL = 2048
BATCH = 16
SEQ = 2048
DEPTH = 2

MIX_WIDTH = D_MODEL
CONV_CH = MIX_WIDTH // 2
CONV_GROUPS = 8
CONV_K = 31
GDN_HEAD_DIM = 128
GDN_V_HEADS = (MIX_WIDTH // 2) // GDN_HEAD_DIM
GDN_QK_HEADS = GDN_V_HEADS // 2
GDN_K_WIDTH = GDN_QK_HEADS * GDN_HEAD_DIM
GDN_V_WIDTH = GDN_V_HEADS * GDN_HEAD_DIM
SHORT_CONV_K = 4
CHUNK = 64
D_FF = ((8 * D_MODEL // 3 + 255) // 256) * 256
FFN_CONV_K = 3
EPS = 1e-6
IN_WIDTH = 2 * CONV_CH + 2 * GDN_K_WIDTH + 2 * GDN_V_WIDTH + 2 * GDN_V_HEADS

kernel_name = "hymba_style_conformer_conv_gated_deltanet_convffn"


def rms_norm(x, g):
    xf = x.astype(jnp.float32)
    y = xf * lax.rsqrt(jnp.mean(xf * xf, axis=-1, keepdims=True) + EPS)
    return (y * g.astype(jnp.float32)).astype(x.dtype)


def group_layer_norm(x, g, b, groups):
    B, S, C = x.shape
    xf = x.astype(jnp.float32).reshape(B, S, groups, C // groups)
    mu = jnp.mean(xf, axis=-1, keepdims=True)
    xc = xf - mu
    var = jnp.mean(xc * xc, axis=-1, keepdims=True)
    y = (xc * lax.rsqrt(var + EPS)).reshape(B, S, C)
    return (y * g.astype(jnp.float32) + b.astype(jnp.float32)).astype(x.dtype)


def causal_depthwise_conv(x, w):
    K, C = w.shape
    return lax.conv_general_dilated(
        x, w[:, None, :].astype(x.dtype), window_strides=(1,), padding=[(K - 1, 0)],
        dimension_numbers=("NWC", "WIO", "NWC"), feature_group_count=C)


def l2_normalize(t):
    return t * lax.rsqrt(jnp.sum(t * t, axis=-1, keepdims=True) + EPS)


def chunk_gated_delta_rule(q, k, v, g, beta):
    B, S, H, dk = q.shape
    dv = v.shape[-1]
    N = S // CHUNK
    q = q * (dk ** -0.5)

    def to_chunks(t):
        return t.reshape(B, N, CHUNK, H, t.shape[-1]).transpose(0, 3, 1, 2, 4)

    q, k, v = to_chunks(q), to_chunks(k), to_chunks(v)
    g = g.reshape(B, N, CHUNK, H).transpose(0, 3, 1, 2)
    beta = beta.reshape(B, N, CHUNK, H).transpose(0, 3, 1, 2)
    g = jnp.cumsum(g, axis=-1)

    causal = jnp.tril(jnp.ones((CHUNK, CHUNK), dtype=bool))
    strict = jnp.tril(jnp.ones((CHUNK, CHUNK), dtype=bool), -1)
    decay = jnp.exp(jnp.where(causal, g[..., :, None] - g[..., None, :], -jnp.inf))

    k_beta = k * beta[..., None]
    v_beta = v * beta[..., None]
    L = jnp.where(strict, jnp.einsum("bhnid,bhnjd->bhnij", k_beta, k) * decay, 0.0)
    A = L + jnp.eye(CHUNK, dtype=L.dtype)
    rhs = jnp.concatenate([v_beta, k_beta * jnp.exp(g)[..., None]], axis=-1)
    sol = lax_linalg.triangular_solve(A, rhs, left_side=True, lower=True, unit_diagonal=True)
    u, w = sol[..., :dv], sol[..., dv:]
    qk = jnp.einsum("bhnid,bhnjd->bhnij", q, k) * decay

    def step(state, inp):
        q_c, k_c, u_c, w_c, g_c, qk_c = inp
        v_new = u_c - jnp.einsum("bhck,bhkv->bhcv", w_c, state)
        o_c = jnp.einsum("bhck,bhkv->bhcv", q_c * jnp.exp(g_c)[..., None], state) + \
            jnp.einsum("bhij,bhjv->bhiv", qk_c, v_new)
        g_last = g_c[..., -1]
        k_dec = k_c * jnp.exp(g_last[..., None] - g_c)[..., None]
        state = state * jnp.exp(g_last)[..., None, None] + jnp.einsum("bhck,bhcv->bhkv", k_dec, v_new)
        return state, o_c

    xs = tuple(jnp.moveaxis(t, 2, 0) for t in (q, k, u, w, g, qk))
    state0 = jnp.zeros((B, H, dk, dv), jnp.float32)
    _, o = lax.scan(step, state0, xs)
    return o.transpose(1, 0, 3, 2, 4).reshape(B, S, H, dv)


def conformer_conv_group(a_val, a_gate, dw_w, dw_b, ln_g, ln_b, pw_w, pw_b):
    u = a_val * jax.nn.sigmoid(a_gate)
    u = causal_depthwise_conv(u, dw_w) + dw_b
    u = group_layer_norm(u, ln_g, ln_b, CONV_GROUPS)
    u = jax.nn.silu(u)
    return u @ pw_w + pw_b


def gated_deltanet_group(q, k, v, z, b_raw, a_raw, conv_w, a_log, dt_bias, norm_g):
    B, S, _ = q.shape
    dtype = q.dtype
    qkv = jax.nn.silu(causal_depthwise_conv(jnp.concatenate([q, k, v], axis=-1), conv_w))
    q, k, v = jnp.split(qkv, [GDN_K_WIDTH, 2 * GDN_K_WIDTH], axis=-1)
    rep = GDN_V_HEADS // GDN_QK_HEADS
    q = jnp.repeat(q.reshape(B, S, GDN_QK_HEADS, GDN_HEAD_DIM), rep, axis=2).astype(jnp.float32)
    k = jnp.repeat(k.reshape(B, S, GDN_QK_HEADS, GDN_HEAD_DIM), rep, axis=2).astype(jnp.float32)
    v = v.reshape(B, S, GDN_V_HEADS, GDN_HEAD_DIM).astype(jnp.float32)
    q, k = l2_normalize(q), l2_normalize(k)
    beta = jax.nn.sigmoid(b_raw.astype(jnp.float32))
    g = -jnp.exp(a_log.astype(jnp.float32)) * jax.nn.softplus(
        a_raw.astype(jnp.float32) + dt_bias.astype(jnp.float32))
    o = chunk_gated_delta_rule(q, k, v, g, beta)
    o = o * lax.rsqrt(jnp.mean(o * o, axis=-1, keepdims=True) + EPS) * norm_g.astype(jnp.float32)
    o = o * jax.nn.silu(z.reshape(B, S, GDN_V_HEADS, GDN_HEAD_DIM).astype(jnp.float32))
    return o.reshape(B, S, GDN_V_WIDTH).astype(dtype)


def setup_inputs(seed: int = 0) -> dict:
    key = jax.random.key(seed)
    ks = jax.random.split(key, 24)
    L = DEPTH

    def nrm(k, shape, scale):
        return jax.random.normal(k, shape, jnp.float32) * scale

    dt = jnp.exp(jax.random.uniform(ks[10], (L, GDN_V_HEADS), jnp.float32,
                                    np.log(1e-3).astype(np.float32), np.log(1e-1).astype(np.float32)))
    return {
        "x": nrm(ks[0], (BATCH, SEQ, D_MODEL), 1.0),
        "mix_norm_g": 1.0 + nrm(ks[1], (L, D_MODEL), 0.02),
        "w_in": nrm(ks[2], (L, D_MODEL, IN_WIDTH), D_MODEL ** -0.5),
        "conv_dw_w": nrm(ks[3], (L, CONV_K, CONV_CH), CONV_K ** -0.5),
        "conv_dw_b": nrm(ks[4], (L, CONV_CH), 0.02),
        "conv_ln_g": 1.0 + nrm(ks[5], (L, CONV_CH), 0.02),
        "conv_ln_b": nrm(ks[6], (L, CONV_CH), 0.02),
        "conv_pw_w": nrm(ks[7], (L, CONV_CH, CONV_CH), CONV_CH ** -0.5),
        "conv_pw_b": nrm(ks[8], (L, CONV_CH), 0.02),
        "gdn_conv_w": nrm(ks[9], (L, SHORT_CONV_K, 2 * GDN_K_WIDTH + GDN_V_WIDTH), SHORT_CONV_K ** -0.5),
        "gdn_a_log": jnp.log(jax.random.uniform(ks[11], (L, GDN_V_HEADS), jnp.float32, 1.0, 16.0)),
        "gdn_dt_bias": dt + jnp.log(-jnp.expm1(-dt)),
        "gdn_norm_g": 1.0 + nrm(ks[12], (L, GDN_HEAD_DIM), 0.02),
        "w_out": nrm(ks[13], (L, MIX_WIDTH, D_MODEL), MIX_WIDTH ** -0.5),
        "ffn_norm_g": 1.0 + nrm(ks[14], (L, D_MODEL), 0.02),
        "w_up": nrm(ks[15], (L, D_MODEL, 2 * D_FF), D_MODEL ** -0.5),
        "ffn_conv_w": nrm(ks[16], (L, FFN_CONV_K, D_FF), FFN_CONV_K ** -0.5),
        "ffn_conv_b": nrm(ks[17], (L, D_FF), 0.02),
        "w_down": nrm(ks[18], (L, D_FF, D_MODEL), D_FF ** -0.5),
        "final_norm_g": 1.0 + nrm(ks[19], (D_MODEL,), 0.02),
    }


def reference(x, mix_norm_g, w_in, conv_dw_w, conv_dw_b, conv_ln_g, conv_ln_b, conv_pw_w, conv_pw_b,
              gdn_conv_w, gdn_a_log, gdn_dt_bias, gdn_norm_g, w_out, ffn_norm_g, w_up,
              ffn_conv_w, ffn_conv_b, w_down, final_norm_g):
    splits = np.cumsum([CONV_CH, CONV_CH, GDN_K_WIDTH, GDN_K_WIDTH, GDN_V_WIDTH, GDN_V_WIDTH,
                        GDN_V_HEADS]).tolist()
    for l in range(DEPTH):
        h = rms_norm(x, mix_norm_g[l])
        p = h @ w_in[l]
        a_val, a_gate, q, k, v, z, b_raw, a_raw = jnp.split(p, splits, axis=-1)
        out_a = conformer_conv_group(a_val, a_gate, conv_dw_w[l], conv_dw_b[l], conv_ln_g[l],
                                     conv_ln_b[l], conv_pw_w[l], conv_pw_b[l])
        out_b = gated_deltanet_group(q, k, v, z, b_raw, a_raw, gdn_conv_w[l], gdn_a_log[l],
                                     gdn_dt_bias[l], gdn_norm_g[l])
        x = x + jnp.concatenate([out_a, out_b], axis=-1) @ w_out[l]
        h = rms_norm(x, ffn_norm_g[l])
        gate, up = jnp.split(h @ w_up[l], 2, axis=-1)
        gate = causal_depthwise_conv(gate, ffn_conv_w[l]) + ffn_conv_b[l]
        x = x + (jax.nn.silu(gate) * up) @ w_down[l]
    return rms_norm(x, final_norm_g)
```

```python
import functools

import jax
import jax.numpy as jnp
from jax import lax
from jax.experimental import pallas as pl
from jax.experimental.pallas import tpu as pltpu

F32 = jnp.float32
BF16 = jnp.bfloat16
EPS = 1e-6

LANES = 128
SUBLANES = 8
CONV_CH = 1024
CONV_GROUPS = 8
CONV_K = 31
HEAD = 128
V_HEADS = 8
QK_HEADS = 4
K_WIDTH = QK_HEADS * HEAD
V_WIDTH = V_HEADS * HEAD
SHORT_K = 4
CHUNK = 64
FFN_K = 3
MAIN_WIDTH = 2 * CONV_CH + 2 * K_WIDTH + 2 * V_WIDTH
BA_WIDTH = 2 * LANES
CONV_HIST = 32
VMEM_LIMIT = 56 * 1024 * 1024


def _dot(a, b):
    return jnp.dot(a, b, preferred_element_type=F32)


def _rms(x, g):
    return x * lax.rsqrt(jnp.mean(x * x, axis=-1, keepdims=True) + EPS) * g


def _sigmoid(x):
    return 1.0 / (1.0 + jnp.exp(-x))


def _split3(x):
    x1 = x.astype(BF16)
    r1 = x - x1.astype(F32)
    x2 = r1.astype(BF16)
    r2 = r1 - x2.astype(F32)
    return x1, x2, r2.astype(BF16)


def _dot_sel(sel_bf16, x):
    x1, x2, x3 = _split3(x)
    return _dot(sel_bf16, x1) + _dot(sel_bf16, x2) + _dot(sel_bf16, x3)


def _mm(a, b, passes):
    if passes == 1:
        return _dot(a.astype(BF16), b.astype(BF16))
    ah = a.astype(BF16)
    al = (a - ah.astype(F32)).astype(BF16)
    bh = b.astype(BF16)
    bl = (b - bh.astype(F32)).astype(BF16)
    return _dot(ah, bh) + _dot(ah, bl) + _dot(al, bh)


def _inproj_kernel(x_ref, g_ref, w_ref, wba_ref, p_ref, ba_ref, h_ref):
    @pl.when(pl.program_id(1) == 0)
    def _():
        h = _rms(x_ref[...], g_ref[...]).astype(BF16)
        h_ref[...] = h
        ba_ref[...] = _dot(h, wba_ref[...])

    p_ref[...] = _dot(h_ref[...], w_ref[...])


def _inproj(x2d, g, w_main, w_ba, *, tm, tn):
    t, d = x2d.shape
    n = w_main.shape[1]
    return pl.pallas_call(
        _inproj_kernel,
        grid=(t // tm, n // tn),
        in_specs=[
            pl.BlockSpec((tm, d), lambda i, j: (i, 0)),
            pl.BlockSpec((1, d), lambda i, j: (0, 0)),
            pl.BlockSpec((d, tn), lambda i, j: (0, j)),
            pl.BlockSpec((d, BA_WIDTH), lambda i, j: (0, 0)),
        ],
        out_specs=[
            pl.BlockSpec((tm, tn), lambda i, j: (i, j)),
            pl.BlockSpec((tm, BA_WIDTH), lambda i, j: (i, 0)),
        ],
        out_shape=[
            jax.ShapeDtypeStruct((t, n), F32),
            jax.ShapeDtypeStruct((t, BA_WIDTH), F32),
        ],
        scratch_shapes=[pltpu.VMEM((tm, d), BF16)],
        compiler_params=pltpu.CompilerParams(
            dimension_semantics=("parallel", "arbitrary"),
            vmem_limit_bytes=VMEM_LIMIT),
        name="inproj",
    )(x2d, g, w_main, w_ba)


def _conf_kernel(av_ref, ag_ref, dww_ref, dwb_ref, lng_ref, lnb_ref, pww_ref, pwb_ref,
                 o_ref, ubuf_ref, *, ts):
    @pl.when(pl.program_id(1) == 0)
    def _():
        ubuf_ref[0:CONV_HIST, :] = jnp.zeros((CONV_HIST, CONV_CH), F32)

    ubuf_ref[CONV_HIST:CONV_HIST + ts, :] = av_ref[...] * _sigmoid(ag_ref[...])

    base = CONV_HIST - (CONV_K - 1)
    acc = dww_ref[0:1, :] * ubuf_ref[base:base + ts, :]
    for j in range(1, CONV_K):
        acc = acc + dww_ref[j:j + 1, :] * ubuf_ref[base + j:base + j + ts, :]
    acc = acc + dwb_ref[...]

    ubuf_ref[0:CONV_HIST, :] = ubuf_ref[ts:ts + CONV_HIST, :]

    cg = CONV_CH // CONV_GROUPS
    parts = []
    for gi in range(CONV_GROUPS):
        xg = acc[:, gi * cg:(gi + 1) * cg]
        xc = xg - jnp.mean(xg, axis=-1, keepdims=True)
        var = jnp.mean(xc * xc, axis=-1, keepdims=True)
        parts.append(xc * lax.rsqrt(var + EPS))
    y = jnp.concatenate(parts, axis=-1) * lng_ref[...] + lnb_ref[...]
    y = y * _sigmoid(y)
    o_ref[...] = (_dot(y.astype(BF16), pww_ref[...]) + pwb_ref[...]).astype(BF16)


def _conformer(p_main, dww, dwb, lng, lnb, pww, pwb, *, batch, seq, ts):
    t = p_main.shape[0]
    ns = seq // ts
    vec = lambda: pl.BlockSpec((1, CONV_CH), lambda b, s: (0, 0))
    return pl.pallas_call(
        functools.partial(_conf_kernel, ts=ts),
        grid=(batch, ns),
        in_specs=[
            pl.BlockSpec((ts, CONV_CH), lambda b, s: (b * ns + s, 0)),
            pl.BlockSpec((ts, CONV_CH), lambda b, s: (b * ns + s, 1)),
            pl.BlockSpec((CONV_K, CONV_CH), lambda b, s: (0, 0)),
            vec(), vec(), vec(),
            pl.BlockSpec((CONV_CH, CONV_CH), lambda b, s: (0, 0)),
            vec(),
        ],
        out_specs=pl.BlockSpec((ts, CONV_CH), lambda b, s: (b * ns + s, 0)),
        out_shape=jax.ShapeDtypeStruct((t, CONV_CH), BF16),
        scratch_shapes=[pltpu.VMEM((CONV_HIST + ts, CONV_CH), F32)],
        compiler_params=pltpu.CompilerParams(
            dimension_semantics=("parallel", "arbitrary"),
            vmem_limit_bytes=VMEM_LIMIT),
        name="conformer",
    )(p_main, p_main, dww, dwb, lng, lnb, pww, pwb)


def _unit_lower_inverse(low, eye, passes):
    inv = eye - low
    power = low
    span = 2
    while span < CHUNK:
        power = _mm(power, power, passes)
        inv = inv + _mm(inv, power, passes)
        span *= 2
    power = _mm(power, power, passes)
    return inv + _mm(inv, power, passes)


def _gdn_kernel(qk_ref, v_ref, z_ref, ba_ref, cw_ref, alog_ref, dtb_ref, ng_ref,
                o_ref, xbuf_ref, state_ref, *, ts, passes):
    nc = ts // CHUNK
    qkv_w = 2 * K_WIDTH + V_WIDTH

    @pl.when(pl.program_id(1) == 0)
    def _():
        xbuf_ref[0:SUBLANES, :] = jnp.zeros((SUBLANES, qkv_w), F32)
        state_ref[...] = jnp.zeros(state_ref.shape, F32)

    xbuf_ref[SUBLANES:SUBLANES + ts, 0:2 * K_WIDTH] = qk_ref[...]
    xbuf_ref[SUBLANES:SUBLANES + ts, 2 * K_WIDTH:qkv_w] = v_ref[...]
    base = SUBLANES - (SHORT_K - 1)
    y = cw_ref[0:1, :] * xbuf_ref[base:base + ts, :]
    for j in range(1, SHORT_K):
        y = y + cw_ref[j:j + 1, :] * xbuf_ref[base + j:base + j + ts, :]
    xbuf_ref[0:SUBLANES, :] = xbuf_ref[ts:ts + SUBLANES, :]
    y = y * _sigmoid(y)

    beta = _sigmoid(ba_ref[:, 0:LANES])
    araw = ba_ref[:, LANES:2 * LANES]
    g = -jnp.exp(alog_ref[...]) * jax.nn.softplus(araw + dtb_ref[...])

    ri = lax.broadcasted_iota(jnp.int32, (ts, ts), 0)
    ci = lax.broadcasted_iota(jnp.int32, (ts, ts), 1)
    shift = CHUNK.bit_length() - 1
    same = (ri >> shift) == (ci >> shift)
    causal = jnp.logical_and(same, ri >= ci)
    strict = jnp.logical_and(same, ri > ci)
    eye = jnp.where(ri == ci, 1.0, 0.0).astype(F32)
    sel = jnp.concatenate([jnp.where(causal, 1.0, 0.0), jnp.where(same, 1.0, 0.0)],
                          axis=0).astype(BF16)
    gsum = _dot_sel(sel, g)
    gc_col = gsum[0:ts]
    gl_col = gsum[ts:2 * ts]
    gc_row = gc_col.T
    gl_row = gl_col.T
    eg_col = jnp.exp(gc_col)
    kd_row = jnp.exp(gl_row - gc_row)
    egl_row = jnp.exp(gl_row)

    scale = HEAD ** -0.5
    kts = []
    ks = []
    qs = []
    for kh in range(QK_HEADS):
        q = y[:, kh * HEAD:(kh + 1) * HEAD]
        k = y[:, K_WIDTH + kh * HEAD:K_WIDTH + (kh + 1) * HEAD]
        q = q * lax.rsqrt(jnp.sum(q * q, axis=-1, keepdims=True) + EPS)
        k = k * lax.rsqrt(jnp.sum(k * k, axis=-1, keepdims=True) + EPS)
        qs.append(q * scale)
        ks.append(k)
        kts.append(k.T)

    for h in range(V_HEADS):
        kh = h // (V_HEADS // QK_HEADS)
        q, k, kt = qs[kh], ks[kh], kts[kh]
        v = y[:, 2 * K_WIDTH + h * HEAD:2 * K_WIDTH + (h + 1) * HEAD]
        gcol = gc_col[:, h:h + 1]
        grow = gc_row[h:h + 1, :]
        bcol = beta[:, h:h + 1]
        egc = eg_col[:, h:h + 1]

        diff = gcol - grow
        decay = jnp.exp(jnp.where(causal, diff, -jnp.inf))
        kb = k * bcol
        kt16 = kt.astype(BF16)
        low = jnp.where(strict, _dot(kb.astype(BF16), kt16) * decay, 0.0)
        qk = _dot(q.astype(BF16), kt16) * decay
        inv = _unit_lower_inverse(low, eye, passes)
        rhs = jnp.concatenate([v * bcol, kb * egc], axis=1)
        uw = _mm(inv, rhs, passes)
        u = uw[:, 0:HEAD]
        w = uw[:, HEAD:2 * HEAD]
        qg = q * egc
        kdt = (kt * kd_row[h:h + 1, :]).astype(BF16)
        qk16 = qk.astype(BF16)

        st = state_ref[h]
        outs = []
        for c in range(nc):
            r0, r1 = c * CHUNK, (c + 1) * CHUNK
            lhs = jnp.concatenate([w[r0:r1], qg[r0:r1]], axis=0).astype(BF16)
            ps = _dot(lhs, st.astype(BF16))
            vnew = u[r0:r1] - ps[0:CHUNK]
            pieces = []
            if r0 > 0:
                pieces.append(jnp.zeros((r0, HEAD), F32))
            pieces.append(vnew)
            if r1 < ts:
                pieces.append(jnp.zeros((ts - r1, HEAD), F32))
            vfull = jnp.concatenate(pieces, axis=0).astype(BF16)
            rs = _dot(jnp.concatenate([qk16[r0:r1], kdt], axis=0), vfull)
            outs.append(ps[CHUNK:2 * CHUNK] + rs[0:CHUNK])
            st = st * egl_row[h:h + 1, r0:r0 + 1] + rs[CHUNK:CHUNK + HEAD]
        state_ref[h] = st

        o = jnp.concatenate(outs, axis=0)
        o = _rms(o, ng_ref[...])
        zz = z_ref[:, h * HEAD:(h + 1) * HEAD]
        o_ref[:, h * HEAD:(h + 1) * HEAD] = (o * (zz * _sigmoid(zz))).astype(BF16)


def _gdn(p_main, p_ba, cw, alog, dtb, ng, *, batch, seq, ts, passes):
    t = p_main.shape[0]
    ns = seq // ts
    qkv_w = 2 * K_WIDTH + V_WIDTH
    row = lambda b, s: b * ns + s
    return pl.pallas_call(
        functools.partial(_gdn_kernel, ts=ts, passes=passes),
        grid=(batch, ns),
        in_specs=[
            pl.BlockSpec((ts, 2 * K_WIDTH), lambda b, s: (row(b, s), 2)),
            pl.BlockSpec((ts, V_WIDTH), lambda b, s: (row(b, s), 3)),
            pl.BlockSpec((ts, V_WIDTH), lambda b, s: (row(b, s), 4)),
            pl.BlockSpec((ts, BA_WIDTH), lambda b, s: (row(b, s), 0)),
            pl.BlockSpec((SHORT_K, qkv_w), lambda b, s: (0, 0)),
            pl.BlockSpec((1, LANES), lambda b, s: (0, 0)),
            pl.BlockSpec((1, LANES), lambda b, s: (0, 0)),
            pl.BlockSpec((1, HEAD), lambda b, s: (0, 0)),
        ],
        out_specs=pl.BlockSpec((ts, V_WIDTH), lambda b, s: (row(b, s), 0)),
        out_shape=jax.ShapeDtypeStruct((t, V_WIDTH), BF16),
        scratch_shapes=[
            pltpu.VMEM((SUBLANES + ts, qkv_w), F32),
            pltpu.VMEM((V_HEADS, HEAD, HEAD), F32),
        ],
        compiler_params=pltpu.CompilerParams(
            dimension_semantics=("parallel", "arbitrary"),
            vmem_limit_bytes=VMEM_LIMIT),
        name="gdn",
    )(p_main, p_main, p_main, p_ba, cw, alog, dtb, ng)


def _outproj_kernel(x_ref, oa_ref, ob_ref, wa_ref, wb_ref, o_ref):
    o_ref[...] = x_ref[...] + _dot(oa_ref[...], wa_ref[...]) + _dot(ob_ref[...], wb_ref[...])


def _outproj(x2d, oa, ob, wa, wb, *, tm):
    t, d = x2d.shape
    return pl.pallas_call(
        _outproj_kernel,
        grid=(t // tm,),
        in_specs=[
            pl.BlockSpec((tm, d), lambda i: (i, 0)),
            pl.BlockSpec((tm, CONV_CH), lambda i: (i, 0)),
            pl.BlockSpec((tm, V_WIDTH), lambda i: (i, 0)),
            pl.BlockSpec((CONV_CH, d), lambda i: (0, 0)),
            pl.BlockSpec((V_WIDTH, d), lambda i: (0, 0)),
        ],
        out_specs=pl.BlockSpec((tm, d), lambda i: (i, 0)),
        out_shape=jax.ShapeDtypeStruct((t, d), F32),
        compiler_params=pltpu.CompilerParams(
            dimension_semantics=("parallel",),
            vmem_limit_bytes=VMEM_LIMIT),
        name="outproj",
    )(x2d, oa, ob, wa, wb)


def _ffn_kernel(x_ref, g_ref, wg_ref, wu_ref, cw_ref, cb_ref, wd_ref, fg_ref,
                o_ref, h_ref, acc_ref, gbuf_ref, carry_ref, *, tm, nf, final):
    s = pl.program_id(1)
    f = pl.program_id(2)

    @pl.when(f == 0)
    def _():
        x = x_ref[...]
        h_ref[...] = _rms(x, g_ref[...]).astype(BF16)
        acc_ref[...] = x

    h = h_ref[...]
    gate = _dot(h, wg_ref[...])
    up = _dot(h, wu_ref[...])

    @pl.when(s == 0)
    def _():
        gbuf_ref[0:SUBLANES, :] = jnp.zeros((SUBLANES, gate.shape[1]), F32)

    @pl.when(s > 0)
    def _():
        gbuf_ref[0:SUBLANES, :] = carry_ref[f]

    gbuf_ref[SUBLANES:SUBLANES + tm, :] = gate
    carry_ref[f] = gate[tm - SUBLANES:tm, :]
    base = SUBLANES - (FFN_K - 1)
    gc = cw_ref[FFN_K - 1:FFN_K, :] * gate + cb_ref[...]
    for j in range(FFN_K - 1):
        gc = gc + cw_ref[j:j + 1, :] * gbuf_ref[base + j:base + j + tm, :]
    act = gc * _sigmoid(gc) * up
    acc_ref[...] += _dot(act.astype(BF16), wd_ref[...])

    @pl.when(f == nf - 1)
    def _():
        out = acc_ref[...]
        if final:
            out = _rms(out, fg_ref[...])
        o_ref[...] = out


def _ffn(x2d, g, w_up, cw, cb, w_down, fg, *, batch, seq, tm, tf, final):
    t, d = x2d.shape
    dff = w_down.shape[0]
    ns = seq // tm
    nf = dff // tf
    return pl.pallas_call(
        functools.partial(_ffn_kernel, tm=tm, nf=nf, final=final),
        grid=(batch, ns, nf),
        in_specs=[
            pl.BlockSpec((tm, d), lambda b, s, f: (b * ns + s, 0)),
            pl.BlockSpec((1, d), lambda b, s, f: (0, 0)),
            pl.BlockSpec((d, tf), lambda b, s, f: (0, f)),
            pl.BlockSpec((d, tf), lambda b, s, f: (0, nf + f)),
            pl.BlockSpec((FFN_K, tf), lambda b, s, f: (0, f)),
            pl.BlockSpec((1, tf), lambda b, s, f: (0, f)),
            pl.BlockSpec((tf, d), lambda b, s, f: (f, 0)),
            pl.BlockSpec((1, d), lambda b, s, f: (0, 0)),
        ],
        out_specs=pl.BlockSpec((tm, d), lambda b, s, f: (b * ns + s, 0)),
        out_shape=jax.ShapeDtypeStruct((t, d), F32),
        scratch_shapes=[
            pltpu.VMEM((tm, d), BF16),
            pltpu.VMEM((tm, d), F32),
            pltpu.VMEM((SUBLANES + tm, tf), F32),
            pltpu.VMEM((nf, SUBLANES, tf), F32),
        ],
        compiler_params=pltpu.CompilerParams(
            dimension_semantics=("parallel", "arbitrary", "arbitrary"),
            vmem_limit_bytes=VMEM_LIMIT),
        name="ffn",
    )(x2d, g, w_up, w_up, cw, cb, w_down, fg)


def _pad_lanes(vec):
    return jnp.zeros((1, LANES), F32).at[0, 0:vec.shape[0]].set(vec)


def kernel(x, mix_norm_g, w_in, conv_dw_w, conv_dw_b, conv_ln_g, conv_ln_b, conv_pw_w, conv_pw_b,
           gdn_conv_w, gdn_a_log, gdn_dt_bias, gdn_norm_g, w_out, ffn_norm_g, w_up,
           ffn_conv_w, ffn_conv_b, w_down, final_norm_g):
    batch, seq, d = x.shape
    depth = w_in.shape[0]
    t = batch * seq
    x2d = x.reshape(t, d)
    fg = final_norm_g.reshape(1, d)
    b_off = MAIN_WIDTH
    a_off = MAIN_WIDTH + V_HEADS

    for l in range(depth):
        w_main = w_in[l, :, 0:MAIN_WIDTH].astype(BF16)
        w_ba = jnp.zeros((d, BA_WIDTH), F32)
        w_ba = w_ba.at[:, 0:V_HEADS].set(w_in[l, :, b_off:b_off + V_HEADS])
        w_ba = w_ba.at[:, LANES:LANES + V_HEADS].set(w_in[l, :, a_off:a_off + V_HEADS])
        p_main, p_ba = _inproj(x2d, mix_norm_g[l].reshape(1, d), w_main, w_ba.astype(BF16),
                               tm=min(1024, seq), tn=1024)
        out_a = _conformer(p_main, conv_dw_w[l], conv_dw_b[l].reshape(1, -1),
                           conv_ln_g[l].reshape(1, -1), conv_ln_b[l].reshape(1, -1),
                           conv_pw_w[l].astype(BF16), conv_pw_b[l].reshape(1, -1),
                           batch=batch, seq=seq, ts=min(512, seq))
        out_b = _gdn(p_main, p_ba, gdn_conv_w[l], _pad_lanes(gdn_a_log[l]),
                     _pad_lanes(gdn_dt_bias[l]), gdn_norm_g[l].reshape(1, -1),
                     batch=batch, seq=seq, ts=256, passes=3)
        wo = w_out[l].astype(BF16)
        x2d = _outproj(x2d, out_a, out_b, wo[0:CONV_CH], wo[CONV_CH:], tm=min(512, seq))
        x2d = _ffn(x2d, ffn_norm_g[l].reshape(1, d), w_up[l].astype(BF16), ffn_conv_w[l],
                   ffn_conv_b[l].reshape(1, -1), w_down[l].astype(BF16), fg,
                   batch=batch, seq=seq, tm=min(512, seq), tf=512, final=(l == depth - 1))
    return x2d.reshape(batch, seq, d)
```

```python
import functools

import jax
import jax.numpy as jnp
from jax import lax
from jax.experimental import pallas as pl
from jax.experimental.pallas import tpu as pltpu

F32 = jnp.float32
BF16 = jnp.bfloat16
EPS = 1e-6

LANES = 128
SUBLANES = 8
CONV_CH = 1024
CONV_GROUPS = 8
CONV_K = 31
HEAD = 128
V_HEADS = 8
QK_HEADS = 4
K_WIDTH = QK_HEADS * HEAD
V_WIDTH = V_HEADS * HEAD
SHORT_K = 4
CHUNK = 64
GROUP = 128
FFN_K = 3
MAIN_WIDTH = 2 * CONV_CH + 2 * K_WIDTH + 2 * V_WIDTH
BA_WIDTH = 2 * LANES
CONV_ROWS = 64
CONV_HIST = 32
VMEM_LIMIT = 56 * 1024 * 1024


def _dot(a, b):
    return jnp.dot(a, b, preferred_element_type=F32)


def _rms(x, g):
    return x * lax.rsqrt(jnp.mean(x * x, axis=-1, keepdims=True) + EPS) * g


def _sigmoid(x):
    return 1.0 / (1.0 + jnp.exp(-x))


def _split3(x):
    x1 = x.astype(BF16)
    r1 = x - x1.astype(F32)
    x2 = r1.astype(BF16)
    r2 = r1 - x2.astype(F32)
    return x1, x2, r2.astype(BF16)


def _dot_sel(sel_bf16, x):
    x1, x2, x3 = _split3(x)
    return _dot(sel_bf16, x1) + _dot(sel_bf16, x2) + _dot(sel_bf16, x3)


def _mm(a, b, passes):
    if passes == 1:
        return _dot(a.astype(BF16), b.astype(BF16))
    ah = a.astype(BF16)
    al = (a - ah.astype(F32)).astype(BF16)
    bh = b.astype(BF16)
    bl = (b - bh.astype(F32)).astype(BF16)
    return _dot(ah, bh) + _dot(ah, bl) + _dot(al, bh)


def _inproj_kernel(x_ref, g_ref, w_ref, wba_ref, p_ref, ba_ref, h_ref):
    @pl.when(pl.program_id(1) == 0)
    def _():
        h = _rms(x_ref[...], g_ref[...]).astype(BF16)
        h_ref[...] = h
        ba_ref[...] = _dot(h, wba_ref[...])

    p_ref[...] = _dot(h_ref[...], w_ref[...])


def _inproj(x2d, g, w_main, w_ba, *, tm, tn):
    t, d = x2d.shape
    n = w_main.shape[1]
    return pl.pallas_call(
        _inproj_kernel,
        grid=(t // tm, n // tn),
        in_specs=[
            pl.BlockSpec((tm, d), lambda i, j: (i, 0)),
            pl.BlockSpec((1, d), lambda i, j: (0, 0)),
            pl.BlockSpec((d, tn), lambda i, j: (0, j)),
            pl.BlockSpec((d, BA_WIDTH), lambda i, j: (0, 0)),
        ],
        out_specs=[
            pl.BlockSpec((tm, tn), lambda i, j: (i, j)),
            pl.BlockSpec((tm, BA_WIDTH), lambda i, j: (i, 0)),
        ],
        out_shape=[
            jax.ShapeDtypeStruct((t, n), F32),
            jax.ShapeDtypeStruct((t, BA_WIDTH), F32),
        ],
        scratch_shapes=[pltpu.VMEM((tm, d), BF16)],
        compiler_params=pltpu.CompilerParams(
            dimension_semantics=("parallel", "arbitrary"),
            vmem_limit_bytes=VMEM_LIMIT),
        name="inproj",
    )(x2d, g, w_main, w_ba)


def _conf_kernel(av_ref, ag_ref, dww_ref, dwb_ref, lng_ref, lnb_ref, pww_ref, pwb_ref,
                 o_ref, ush_ref, y_ref, *, ts):
    rows = CONV_HIST + ts

    @pl.when(pl.program_id(1) == 0)
    def _():
        ush_ref[0, 0:CONV_HIST, :] = jnp.zeros((CONV_HIST, CONV_CH), F32)

    ush_ref[0, CONV_HIST:rows, :] = av_ref[...] * _sigmoid(ag_ref[...])
    for r in range(1, SUBLANES):
        ush_ref[r, SUBLANES:rows, :] = ush_ref[0, SUBLANES - r:rows - r, :]

    cg = CONV_CH // CONV_GROUPS
    for rb in range(ts // CONV_ROWS):
        for gi in range(CONV_GROUPS):
            lanes = slice(gi * cg, (gi + 1) * cg)
            acc = None
            for j in range(CONV_K):
                a, r = divmod(CONV_K - 1 - j, SUBLANES)
                start = CONV_HIST - a * SUBLANES + rb * CONV_ROWS
                term = dww_ref[j:j + 1, lanes] * ush_ref[r, start:start + CONV_ROWS, lanes]
                acc = term if acc is None else acc + term
            acc = acc + dwb_ref[:, lanes]
            xc = acc - jnp.mean(acc, axis=-1, keepdims=True)
            var = jnp.mean(xc * xc, axis=-1, keepdims=True)
            yv = xc * lax.rsqrt(var + EPS) * lng_ref[:, lanes] + lnb_ref[:, lanes]
            y_ref[rb * CONV_ROWS:(rb + 1) * CONV_ROWS, lanes] = (yv * _sigmoid(yv)).astype(BF16)

    ush_ref[0, 0:CONV_HIST, :] = ush_ref[0, ts:rows, :]
    o_ref[...] = (_dot(y_ref[...], pww_ref[...]) + pwb_ref[...]).astype(BF16)


def _conformer(p_main, dww, dwb, lng, lnb, pww, pwb, *, batch, seq, ts):
    t = p_main.shape[0]
    ns = seq // ts
    vec = lambda: pl.BlockSpec((1, CONV_CH), lambda b, s: (0, 0))
    return pl.pallas_call(
        functools.partial(_conf_kernel, ts=ts),
        grid=(batch, ns),
        in_specs=[
            pl.BlockSpec((ts, CONV_CH), lambda b, s: (b * ns + s, 0)),
            pl.BlockSpec((ts, CONV_CH), lambda b, s: (b * ns + s, 1)),
            pl.BlockSpec((CONV_K, CONV_CH), lambda b, s: (0, 0)),
            vec(), vec(), vec(),
            pl.BlockSpec((CONV_CH, CONV_CH), lambda b, s: (0, 0)),
            vec(),
        ],
        out_specs=pl.BlockSpec((ts, CONV_CH), lambda b, s: (b * ns + s, 0)),
        out_shape=jax.ShapeDtypeStruct((t, CONV_CH), BF16),
        scratch_shapes=[
            pltpu.VMEM((SUBLANES, CONV_HIST + ts, CONV_CH), F32),
            pltpu.VMEM((ts, CONV_CH), BF16),
        ],
        compiler_params=pltpu.CompilerParams(
            dimension_semantics=("parallel", "arbitrary"),
            vmem_limit_bytes=VMEM_LIMIT),
        name="conformer",
    )(p_main, p_main, dww, dwb, lng, lnb, pww, pwb)


def _gdn_kernel(qk_ref, v_ref, z_ref, ba_ref, cw_ref, alog_ref, dtb_ref, ng_ref,
                o_ref, xbuf_ref, state_ref, *, ts, passes):
    qkv_w = 2 * K_WIDTH + V_WIDTH

    @pl.when(pl.program_id(1) == 0)
    def _():
        xbuf_ref[0:SUBLANES, :] = jnp.zeros((SUBLANES, qkv_w), F32)
        state_ref[...] = jnp.zeros(state_ref.shape, F32)

    xbuf_ref[SUBLANES:SUBLANES + ts, 0:2 * K_WIDTH] = qk_ref[...]
    xbuf_ref[SUBLANES:SUBLANES + ts, 2 * K_WIDTH:qkv_w] = v_ref[...]
    base = SUBLANES - (SHORT_K - 1)
    y = cw_ref[0:1, :] * xbuf_ref[base:base + ts, :]
    for j in range(1, SHORT_K):
        y = y + cw_ref[j:j + 1, :] * xbuf_ref[base + j:base + j + ts, :]
    xbuf_ref[0:SUBLANES, :] = xbuf_ref[ts:ts + SUBLANES, :]
    y = y * _sigmoid(y)

    beta = _sigmoid(ba_ref[:, 0:LANES])
    araw = ba_ref[:, LANES:2 * LANES]
    g = -jnp.exp(alog_ref[...]) * jax.nn.softplus(araw + dtb_ref[...])

    ri = lax.broadcasted_iota(jnp.int32, (ts, ts), 0)
    ci = lax.broadcasted_iota(jnp.int32, (ts, ts), 1)
    shift = CHUNK.bit_length() - 1
    same_t = (ri >> shift) == (ci >> shift)
    sel = jnp.concatenate([jnp.where(jnp.logical_and(same_t, ri >= ci), 1.0, 0.0),
                           jnp.where(same_t, 1.0, 0.0)], axis=0).astype(BF16)
    gsum = _dot_sel(sel, g)
    gc_col = gsum[0:ts]
    gl_col = gsum[ts:2 * ts]
    gc_row = gc_col.T
    gl_row = gl_col.T
    eg_col = jnp.exp(gc_col)
    kd_row = jnp.exp(gl_row - gc_row)
    egl_row = jnp.exp(gl_row)

    rg = lax.broadcasted_iota(jnp.int32, (GROUP, GROUP), 0)
    cg = lax.broadcasted_iota(jnp.int32, (GROUP, GROUP), 1)
    same = (rg >> shift) == (cg >> shift)
    causal = jnp.logical_and(same, rg >= cg)
    strict = jnp.logical_and(same, rg > cg)
    eye = jnp.where(rg == cg, 1.0, 0.0).astype(F32)

    scale = HEAD ** -0.5
    kts = []
    ks = []
    qs = []
    for kh in range(QK_HEADS):
        q = y[:, kh * HEAD:(kh + 1) * HEAD]
        k = y[:, K_WIDTH + kh * HEAD:K_WIDTH + (kh + 1) * HEAD]
        q = q * lax.rsqrt(jnp.sum(q * q, axis=-1, keepdims=True) + EPS)
        k = k * lax.rsqrt(jnp.sum(k * k, axis=-1, keepdims=True) + EPS)
        qs.append(q * scale)
        ks.append(k)
        kts.append(k.T)

    rep = V_HEADS // QK_HEADS
    ngr = ts // GROUP
    units = [(h, gi) for h in range(V_HEADS) for gi in range(ngr)]
    kt16s = [kt.astype(BF16) for kt in kts]
    lows, qk16s, rhss, qgs, kdts = {}, {}, {}, {}, {}
    for h, gi in units:
        rows = slice(gi * GROUP, (gi + 1) * GROUP)
        q, k, kt16 = qs[h // rep][rows], ks[h // rep][rows], kt16s[h // rep][:, rows]
        v = y[rows, 2 * K_WIDTH + h * HEAD:2 * K_WIDTH + (h + 1) * HEAD]
        bcol = beta[rows, h:h + 1]
        egc = eg_col[rows, h:h + 1]
        decay = jnp.exp(jnp.where(causal, gc_col[rows, h:h + 1] - gc_row[h:h + 1, rows], -jnp.inf))
        kb = k * bcol
        lows[h, gi] = jnp.where(strict, _dot(kb.astype(BF16), kt16) * decay, 0.0)
        qk16s[h, gi] = (_dot(q.astype(BF16), kt16) * decay).astype(BF16)
        rhss[h, gi] = jnp.concatenate([v * bcol, kb * egc], axis=1)
        qgs[h, gi] = q * egc
        kdts[h, gi] = (kts[h // rep][:, rows] * kd_row[h:h + 1, rows]).astype(BF16)

    invs = {u: eye - lows[u] for u in units}
    powers = {u: _mm(lows[u], lows[u], passes) for u in units}
    span = 4
    while span < CHUNK:
        prods = {u: _mm(jnp.concatenate([powers[u], invs[u]], axis=0), powers[u], passes) for u in units}
        powers = {u: prods[u][0:GROUP] for u in units}
        invs = {u: invs[u] + prods[u][GROUP:2 * GROUP] for u in units}
        span *= 2
    invs = {u: invs[u] + _mm(invs[u], powers[u], passes) for u in units}
    uws = {u: _mm(invs[u], rhss[u], passes) for u in units}

    sts = [state_ref[h] for h in range(V_HEADS)]
    outs = [[] for _ in range(V_HEADS)]
    for gi in range(ngr):
        for c in range(GROUP // CHUNK):
            r0, r1 = c * CHUNK, (c + 1) * CHUNK
            t0 = gi * GROUP + r0
            pss = []
            for h in range(V_HEADS):
                lhs = jnp.concatenate([uws[h, gi][r0:r1, HEAD:2 * HEAD], qgs[h, gi][r0:r1]], axis=0)
                pss.append(_dot(lhs.astype(BF16), sts[h].astype(BF16)))
            rss = []
            for h in range(V_HEADS):
                vnew = uws[h, gi][r0:r1, 0:HEAD] - pss[h][0:CHUNK]
                pieces = []
                if r0 > 0:
                    pieces.append(jnp.zeros((r0, HEAD), F32))
                pieces.append(vnew)
                if r1 < GROUP:
                    pieces.append(jnp.zeros((GROUP - r1, HEAD), F32))
                vfull = jnp.concatenate(pieces, axis=0).astype(BF16)
                rss.append(_dot(jnp.concatenate([qk16s[h, gi][r0:r1], kdts[h, gi]], axis=0), vfull))
            for h in range(V_HEADS):
                outs[h].append(pss[h][CHUNK:2 * CHUNK] + rss[h][0:CHUNK])
                sts[h] = sts[h] * egl_row[h:h + 1, t0:t0 + 1] + rss[h][CHUNK:CHUNK + HEAD]

    for h in range(V_HEADS):
        state_ref[h] = sts[h]
        o = _rms(jnp.concatenate(outs[h], axis=0), ng_ref[...])
        zz = z_ref[:, h * HEAD:(h + 1) * HEAD]
        o_ref[:, h * HEAD:(h + 1) * HEAD] = (o * (zz * _sigmoid(zz))).astype(BF16)


def _gdn(p_main, p_ba, cw, alog, dtb, ng, *, batch, seq, ts, passes):
    t = p_main.shape[0]
    ns = seq // ts
    qkv_w = 2 * K_WIDTH + V_WIDTH
    row = lambda b, s: b * ns + s
    return pl.pallas_call(
        functools.partial(_gdn_kernel, ts=ts, passes=passes),
        grid=(batch, ns),
        in_specs=[
            pl.BlockSpec((ts, 2 * K_WIDTH), lambda b, s: (row(b, s), 2)),
            pl.BlockSpec((ts, V_WIDTH), lambda b, s: (row(b, s), 3)),
            pl.BlockSpec((ts, V_WIDTH), lambda b, s: (row(b, s), 4)),
            pl.BlockSpec((ts, BA_WIDTH), lambda b, s: (row(b, s), 0)),
            pl.BlockSpec((SHORT_K, qkv_w), lambda b, s: (0, 0)),
            pl.BlockSpec((1, LANES), lambda b, s: (0, 0)),
            pl.BlockSpec((1, LANES), lambda b, s: (0, 0)),
            pl.BlockSpec((1, HEAD), lambda b, s: (0, 0)),
        ],
        out_specs=pl.BlockSpec((ts, V_WIDTH), lambda b, s: (row(b, s), 0)),
        out_shape=jax.ShapeDtypeStruct((t, V_WIDTH), BF16),
        scratch_shapes=[
            pltpu.VMEM((SUBLANES + ts, qkv_w), F32),
            pltpu.VMEM((V_HEADS, HEAD, HEAD), F32),
        ],
        compiler_params=pltpu.CompilerParams(
            dimension_semantics=("parallel", "arbitrary"),
            vmem_limit_bytes=VMEM_LIMIT),
        name="gdn",
    )(p_main, p_main, p_main, p_ba, cw, alog, dtb, ng)


def _outproj_kernel(x_ref, oa_ref, ob_ref, wa_ref, wb_ref, o_ref):
    o_ref[...] = x_ref[...] + _dot(oa_ref[...], wa_ref[...]) + _dot(ob_ref[...], wb_ref[...])


def _outproj(x2d, oa, ob, wa, wb, *, tm):
    t, d = x2d.shape
    return pl.pallas_call(
        _outproj_kernel,
        grid=(t // tm,),
        in_specs=[
            pl.BlockSpec((tm, d), lambda i: (i, 0)),
            pl.BlockSpec((tm, CONV_CH), lambda i: (i, 0)),
            pl.BlockSpec((tm, V_WIDTH), lambda i: (i, 0)),
            pl.BlockSpec((CONV_CH, d), lambda i: (0, 0)),
            pl.BlockSpec((V_WIDTH, d), lambda i: (0, 0)),
        ],
        out_specs=pl.BlockSpec((tm, d), lambda i: (i, 0)),
        out_shape=jax.ShapeDtypeStruct((t, d), F32),
        compiler_params=pltpu.CompilerParams(
            dimension_semantics=("parallel",),
            vmem_limit_bytes=VMEM_LIMIT),
        name="outproj",
    )(x2d, oa, ob, wa, wb)


def _ffn_kernel(x_ref, g_ref, wg_ref, wu_ref, cw_ref, cb_ref, wd_ref, fg_ref,
                o_ref, h_ref, acc_ref, act_ref, gbuf_ref, carry_ref, *, tm, nf, final):
    s = pl.program_id(1)
    f = pl.program_id(2)

    @pl.when(f == 0)
    def _():
        x = x_ref[...]
        h_ref[...] = _rms(x, g_ref[...]).astype(BF16)
        acc_ref[...] = x
        act_ref[1] = jnp.zeros(act_ref.shape[1:], BF16)

        @pl.when(s == 0)
        def _():
            carry_ref[...] = jnp.zeros(carry_ref.shape, F32)

    def step(slot):
        h = h_ref[...]
        gate = _dot(h, wg_ref[...])
        up = _dot(h, wu_ref[...])
        acc_ref[...] += _dot(act_ref[1 - slot], wd_ref[...])
        gbuf_ref[0:SUBLANES, :] = carry_ref[f]
        gbuf_ref[SUBLANES:SUBLANES + tm, :] = gate
        carry_ref[f] = gate[tm - SUBLANES:tm, :]
        base = SUBLANES - (FFN_K - 1)
        gc = cw_ref[FFN_K - 1:FFN_K, :] * gate + cb_ref[...]
        for j in range(FFN_K - 1):
            gc = gc + cw_ref[j:j + 1, :] * gbuf_ref[base + j:base + j + tm, :]
        act_ref[slot] = (gc * _sigmoid(gc) * up).astype(BF16)

    for slot in range(2):
        pl.when(jnp.logical_and(f < nf, f % 2 == slot))(functools.partial(step, slot))

    @pl.when(f == nf)
    def _():
        out = acc_ref[...] + _dot(act_ref[(nf - 1) % 2], wd_ref[...])
        if final:
            out = _rms(out, fg_ref[...])
        o_ref[...] = out


def _ffn(x2d, g, w_up, cw, cb, w_down, fg, *, batch, seq, tm, tf, final):
    t, d = x2d.shape
    dff = w_down.shape[0]
    ns = seq // tm
    nf = dff // tf
    cur = lambda f: jnp.minimum(f, nf - 1)
    prev = lambda f: jnp.maximum(f - 1, 0)
    return pl.pallas_call(
        functools.partial(_ffn_kernel, tm=tm, nf=nf, final=final),
        grid=(batch, ns, nf + 1),
        in_specs=[
            pl.BlockSpec((tm, d), lambda b, s, f: (b * ns + s, 0)),
            pl.BlockSpec((1, d), lambda b, s, f: (0, 0)),
            pl.BlockSpec((d, tf), lambda b, s, f: (0, cur(f))),
            pl.BlockSpec((d, tf), lambda b, s, f: (0, nf + cur(f))),
            pl.BlockSpec((FFN_K, tf), lambda b, s, f: (0, cur(f))),
            pl.BlockSpec((1, tf), lambda b, s, f: (0, cur(f))),
            pl.BlockSpec((tf, d), lambda b, s, f: (prev(f), 0)),
            pl.BlockSpec((1, d), lambda b, s, f: (0, 0)),
        ],
        out_specs=pl.BlockSpec((tm, d), lambda b, s, f: (b * ns + s, 0)),
        out_shape=jax.ShapeDtypeStruct((t, d), F32),
        scratch_shapes=[
            pltpu.VMEM((tm, d), BF16),
            pltpu.VMEM((tm, d), F32),
            pltpu.VMEM((2, tm, tf), BF16),
            pltpu.VMEM((SUBLANES + tm, tf), F32),
            pltpu.VMEM((nf, SUBLANES, tf), F32),
        ],
        compiler_params=pltpu.CompilerParams(
            dimension_semantics=("parallel", "arbitrary", "arbitrary"),
            vmem_limit_bytes=VMEM_LIMIT),
        name="ffn",
    )(x2d, g, w_up, w_up, cw, cb, w_down, fg)


def _pad_lanes(vec):
    return jnp.zeros((1, LANES), F32).at[0, 0:vec.shape[0]].set(vec)


def kernel(x, mix_norm_g, w_in, conv_dw_w, conv_dw_b, conv_ln_g, conv_ln_b, conv_pw_w, conv_pw_b,
           gdn_conv_w, gdn_a_log, gdn_dt_bias, gdn_norm_g, w_out, ffn_norm_g, w_up,
           ffn_conv_w, ffn_conv_b, w_down, final_norm_g):
    batch, seq, d = x.shape
    depth = w_in.shape[0]
    t = batch * seq
    x2d = x.reshape(t, d)
    fg = final_norm_g.reshape(1, d)
    b_off = MAIN_WIDTH
    a_off = MAIN_WIDTH + V_HEADS

    for l in range(depth):
        w_main = w_in[l, :, 0:MAIN_WIDTH].astype(BF16)
        w_ba = jnp.zeros((d, BA_WIDTH), F32)
        w_ba = w_ba.at[:, 0:V_HEADS].set(w_in[l, :, b_off:b_off + V_HEADS])
        w_ba = w_ba.at[:, LANES:LANES + V_HEADS].set(w_in[l, :, a_off:a_off + V_HEADS])
        p_main, p_ba = _inproj(x2d, mix_norm_g[l].reshape(1, d), w_main, w_ba.astype(BF16),
                               tm=min(1024, seq), tn=1024)
        out_a = _conformer(p_main, conv_dw_w[l], conv_dw_b[l].reshape(1, -1),
                           conv_ln_g[l].reshape(1, -1), conv_ln_b[l].reshape(1, -1),
                           conv_pw_w[l].astype(BF16), conv_pw_b[l].reshape(1, -1),
                           batch=batch, seq=seq, ts=min(512, seq))
        out_b = _gdn(p_main, p_ba, gdn_conv_w[l], _pad_lanes(gdn_a_log[l]),
                     _pad_lanes(gdn_dt_bias[l]), gdn_norm_g[l].reshape(1, -1),
                     batch=batch, seq=seq, ts=256, passes=1)
        wo = w_out[l].astype(BF16)
        x2d = _outproj(x2d, out_a, out_b, wo[0:CONV_CH], wo[CONV_CH:], tm=min(512, seq))
        x2d = _ffn(x2d, ffn_norm_g[l].reshape(1, d), w_up[l].astype(BF16), ffn_conv_w[l],
                   ffn_conv_b[l].reshape(1, -1), w_down[l].astype(BF16), fg,
                   batch=batch, seq=seq, tm=min(512, seq), tf=512, final=(l == depth - 1))
    return x2d.reshape(batch, seq, d)
```

```python
import functools

import jax
import jax.numpy as jnp
from jax import lax
from jax.experimental import pallas as pl
from jax.experimental.pallas import tpu as pltpu

F32 = jnp.float32
BF16 = jnp.bfloat16
EPS = 1e-6

LANES = 128
SUBLANES = 8
CONV_CH = 1024
CONV_GROUPS = 8
CONV_K = 31
HEAD = 128
V_HEADS = 8
QK_HEADS = 4
K_WIDTH = QK_HEADS * HEAD
V_WIDTH = V_HEADS * HEAD
SHORT_K = 4
CHUNK = 64
GROUP = 128
FFN_K = 3
MAIN_WIDTH = 2 * CONV_CH + 2 * K_WIDTH + 2 * V_WIDTH
BA_WIDTH = 2 * LANES
CONV_ROWS = 64
CONV_HIST = 32
VMEM_LIMIT = 56 * 1024 * 1024


def _dot(a, b):
    return jnp.dot(a, b, preferred_element_type=F32)


def _rms(x, g):
    return x * lax.rsqrt(jnp.mean(x * x, axis=-1, keepdims=True) + EPS) * g


def _sigmoid(x):
    return 1.0 / (1.0 + jnp.exp(-x))


def _split3(x):
    x1 = x.astype(BF16)
    r1 = x - x1.astype(F32)
    x2 = r1.astype(BF16)
    r2 = r1 - x2.astype(F32)
    return x1, x2, r2.astype(BF16)


def _dot_sel(sel_bf16, x):
    x1, x2, x3 = _split3(x)
    return _dot(sel_bf16, x1) + _dot(sel_bf16, x2) + _dot(sel_bf16, x3)


def _mm(a, b, passes):
    if passes == 1:
        return _dot(a.astype(BF16), b.astype(BF16))
    ah = a.astype(BF16)
    al = (a - ah.astype(F32)).astype(BF16)
    bh = b.astype(BF16)
    bl = (b - bh.astype(F32)).astype(BF16)
    return _dot(ah, bh) + _dot(ah, bl) + _dot(al, bh)


def _inproj_kernel(x_ref, g_ref, w_ref, wba_ref, p_ref, ba_ref, h_ref):
    @pl.when(pl.program_id(1) == 0)
    def _():
        h = _rms(x_ref[...], g_ref[...]).astype(BF16)
        h_ref[...] = h
        ba_ref[...] = _dot(h, wba_ref[...])

    p_ref[...] = _dot(h_ref[...], w_ref[...]).astype(BF16)


def _inproj(x2d, g, w_main, w_ba, *, tm, tn):
    t, d = x2d.shape
    n = w_main.shape[1]
    return pl.pallas_call(
        _inproj_kernel,
        grid=(t // tm, n // tn),
        in_specs=[
            pl.BlockSpec((tm, d), lambda i, j: (i, 0)),
            pl.BlockSpec((1, d), lambda i, j: (0, 0)),
            pl.BlockSpec((d, tn), lambda i, j: (0, j)),
            pl.BlockSpec((d, BA_WIDTH), lambda i, j: (0, 0)),
        ],
        out_specs=[
            pl.BlockSpec((tm, tn), lambda i, j: (i, j)),
            pl.BlockSpec((tm, BA_WIDTH), lambda i, j: (i, 0)),
        ],
        out_shape=[
            jax.ShapeDtypeStruct((t, n), BF16),
            jax.ShapeDtypeStruct((t, BA_WIDTH), F32),
        ],
        scratch_shapes=[pltpu.VMEM((tm, d), BF16)],
        compiler_params=pltpu.CompilerParams(
            dimension_semantics=("parallel", "arbitrary"),
            vmem_limit_bytes=VMEM_LIMIT),
        name="inproj",
    )(x2d, g, w_main, w_ba)


def _conf_kernel(av_ref, ag_ref, dww_ref, dwb_ref, lng_ref, lnb_ref, pww_ref, pwb_ref,
                 o_ref, ush_ref, y_ref, *, ts):
    rows = CONV_HIST + ts

    @pl.when(pl.program_id(1) == 0)
    def _():
        ush_ref[0, 0:CONV_HIST, :] = jnp.zeros((CONV_HIST, CONV_CH), F32)

    ush_ref[0, CONV_HIST:rows, :] = av_ref[...].astype(F32) * _sigmoid(ag_ref[...].astype(F32))
    u_all = ush_ref[0]
    for r in range(1, SUBLANES):
        ush_ref[r, SUBLANES:rows, :] = pltpu.roll(u_all, r, axis=0)[SUBLANES:rows]

    cg = CONV_CH // CONV_GROUPS
    for rb in range(ts // CONV_ROWS):
        for gi in range(CONV_GROUPS):
            lanes = slice(gi * cg, (gi + 1) * cg)
            acc = None
            for j in range(CONV_K):
                a, r = divmod(CONV_K - 1 - j, SUBLANES)
                start = CONV_HIST - a * SUBLANES + rb * CONV_ROWS
                term = dww_ref[j:j + 1, lanes] * ush_ref[r, start:start + CONV_ROWS, lanes]
                acc = term if acc is None else acc + term
            acc = acc + dwb_ref[:, lanes]
            xc = acc - jnp.mean(acc, axis=-1, keepdims=True)
            var = jnp.mean(xc * xc, axis=-1, keepdims=True)
            yv = xc * lax.rsqrt(var + EPS) * lng_ref[:, lanes] + lnb_ref[:, lanes]
            y_ref[rb * CONV_ROWS:(rb + 1) * CONV_ROWS, lanes] = (yv * _sigmoid(yv)).astype(BF16)

    ush_ref[0, 0:CONV_HIST, :] = ush_ref[0, ts:rows, :]
    o_ref[...] = (_dot(y_ref[...], pww_ref[...]) + pwb_ref[...]).astype(BF16)


def _conformer(p_main, dww, dwb, lng, lnb, pww, pwb, *, batch, seq, ts):
    t = p_main.shape[0]
    ns = seq // ts
    vec = lambda: pl.BlockSpec((1, CONV_CH), lambda b, s: (0, 0))
    return pl.pallas_call(
        functools.partial(_conf_kernel, ts=ts),
        grid=(batch, ns),
        in_specs=[
            pl.BlockSpec((ts, CONV_CH), lambda b, s: (b * ns + s, 0)),
            pl.BlockSpec((ts, CONV_CH), lambda b, s: (b * ns + s, 1)),
            pl.BlockSpec((CONV_K, CONV_CH), lambda b, s: (0, 0)),
            vec(), vec(), vec(),
            pl.BlockSpec((CONV_CH, CONV_CH), lambda b, s: (0, 0)),
            vec(),
        ],
        out_specs=pl.BlockSpec((ts, CONV_CH), lambda b, s: (b * ns + s, 0)),
        out_shape=jax.ShapeDtypeStruct((t, CONV_CH), BF16),
        scratch_shapes=[
            pltpu.VMEM((SUBLANES, CONV_HIST + ts, CONV_CH), F32),
            pltpu.VMEM((ts, CONV_CH), BF16),
        ],
        compiler_params=pltpu.CompilerParams(
            dimension_semantics=("parallel", "arbitrary"),
            vmem_limit_bytes=VMEM_LIMIT),
        name="conformer",
    )(p_main, p_main, dww, dwb, lng, lnb, pww, pwb)


def _gdn_kernel(qk_ref, v_ref, z_ref, ba_ref, cw_ref, alog_ref, dtb_ref, ng_ref,
                o_ref, xbuf_ref, state_ref, *, ts, passes):
    qkv_w = 2 * K_WIDTH + V_WIDTH

    @pl.when(pl.program_id(1) == 0)
    def _():
        xbuf_ref[0:SUBLANES, :] = jnp.zeros((SUBLANES, qkv_w), F32)
        state_ref[...] = jnp.zeros(state_ref.shape, F32)

    xbuf_ref[SUBLANES:SUBLANES + ts, 0:2 * K_WIDTH] = qk_ref[...].astype(F32)
    xbuf_ref[SUBLANES:SUBLANES + ts, 2 * K_WIDTH:qkv_w] = v_ref[...].astype(F32)
    x_all = xbuf_ref[...]
    y = cw_ref[SHORT_K - 1:SHORT_K, :] * x_all[SUBLANES:SUBLANES + ts]
    for j in range(SHORT_K - 1):
        delayed = pltpu.roll(x_all, SHORT_K - 1 - j, axis=0)
        y = y + cw_ref[j:j + 1, :] * delayed[SUBLANES:SUBLANES + ts]
    xbuf_ref[0:SUBLANES, :] = x_all[ts:ts + SUBLANES]
    y = y * _sigmoid(y)

    beta = _sigmoid(ba_ref[:, 0:LANES])
    araw = ba_ref[:, LANES:2 * LANES]
    g = -jnp.exp(alog_ref[...]) * jax.nn.softplus(araw + dtb_ref[...])

    ri = lax.broadcasted_iota(jnp.int32, (ts, ts), 0)
    ci = lax.broadcasted_iota(jnp.int32, (ts, ts), 1)
    shift = CHUNK.bit_length() - 1
    same_t = (ri >> shift) == (ci >> shift)
    sel = jnp.concatenate([jnp.where(jnp.logical_and(same_t, ri >= ci), 1.0, 0.0),
                           jnp.where(same_t, 1.0, 0.0)], axis=0).astype(BF16)
    gsum = _dot_sel(sel, g)
    gc_col = gsum[0:ts]
    gl_col = gsum[ts:2 * ts]
    gc_row = gc_col.T
    gl_row = gl_col.T
    eg_col = jnp.exp(gc_col)
    kd_row = jnp.exp(gl_row - gc_row)
    egl_row = jnp.exp(gl_row)

    rg = lax.broadcasted_iota(jnp.int32, (GROUP, GROUP), 0)
    cg = lax.broadcasted_iota(jnp.int32, (GROUP, GROUP), 1)
    same = (rg >> shift) == (cg >> shift)
    causal = jnp.logical_and(same, rg >= cg)
    strict = jnp.logical_and(same, rg > cg)
    eye = jnp.where(rg == cg, 1.0, 0.0).astype(F32)

    scale = HEAD ** -0.5
    kts = []
    ks = []
    qs = []
    for kh in range(QK_HEADS):
        q = y[:, kh * HEAD:(kh + 1) * HEAD]
        k = y[:, K_WIDTH + kh * HEAD:K_WIDTH + (kh + 1) * HEAD]
        q = q * lax.rsqrt(jnp.sum(q * q, axis=-1, keepdims=True) + EPS)
        k = k * lax.rsqrt(jnp.sum(k * k, axis=-1, keepdims=True) + EPS)
        qs.append(q * scale)
        ks.append(k)
        kts.append(k.T)

    rep = V_HEADS // QK_HEADS
    ngr = ts // GROUP
    units = [(h, gi) for h in range(V_HEADS) for gi in range(ngr)]
    kt16s = [kt.astype(BF16) for kt in kts]
    lows, qk16s, rhss, qgs, kdts = {}, {}, {}, {}, {}
    for h, gi in units:
        rows = slice(gi * GROUP, (gi + 1) * GROUP)
        q, k, kt16 = qs[h // rep][rows], ks[h // rep][rows], kt16s[h // rep][:, rows]
        v = y[rows, 2 * K_WIDTH + h * HEAD:2 * K_WIDTH + (h + 1) * HEAD]
        bcol = beta[rows, h:h + 1]
        egc = eg_col[rows, h:h + 1]
        decay = jnp.exp(jnp.where(causal, gc_col[rows, h:h + 1] - gc_row[h:h + 1, rows], -jnp.inf))
        kb = k * bcol
        lows[h, gi] = jnp.where(strict, _dot(kb.astype(BF16), kt16) * decay, 0.0)
        qk16s[h, gi] = (_dot(q.astype(BF16), kt16) * decay).astype(BF16)
        rhss[h, gi] = jnp.concatenate([v * bcol, kb * egc], axis=1)
        qgs[h, gi] = q * egc
        kdts[h, gi] = (kts[h // rep][:, rows] * kd_row[h:h + 1, rows]).astype(BF16)

    invs = {u: eye - lows[u] for u in units}
    powers = {u: _mm(lows[u], lows[u], passes) for u in units}
    span = 4
    while span < CHUNK:
        prods = {u: _mm(jnp.concatenate([powers[u], invs[u]], axis=0), powers[u], passes) for u in units}
        powers = {u: prods[u][0:GROUP] for u in units}
        invs = {u: invs[u] + prods[u][GROUP:2 * GROUP] for u in units}
        span *= 2
    invs = {u: invs[u] + _mm(invs[u], powers[u], passes) for u in units}
    uws = {u: _mm(invs[u], rhss[u], passes) for u in units}

    sts = [state_ref[h] for h in range(V_HEADS)]
    outs = [[] for _ in range(V_HEADS)]
    for gi in range(ngr):
        for c in range(GROUP // CHUNK):
            r0, r1 = c * CHUNK, (c + 1) * CHUNK
            t0 = gi * GROUP + r0
            pss = []
            for h in range(V_HEADS):
                lhs = jnp.concatenate([uws[h, gi][r0:r1, HEAD:2 * HEAD], qgs[h, gi][r0:r1]], axis=0)
                pss.append(_dot(lhs.astype(BF16), sts[h].astype(BF16)))
            rss = []
            for h in range(V_HEADS):
                vnew = uws[h, gi][r0:r1, 0:HEAD] - pss[h][0:CHUNK]
                pieces = []
                if r0 > 0:
                    pieces.append(jnp.zeros((r0, HEAD), F32))
                pieces.append(vnew)
                if r1 < GROUP:
                    pieces.append(jnp.zeros((GROUP - r1, HEAD), F32))
                vfull = jnp.concatenate(pieces, axis=0).astype(BF16)
                rss.append(_dot(jnp.concatenate([qk16s[h, gi][r0:r1], kdts[h, gi]], axis=0), vfull))
            for h in range(V_HEADS):
                outs[h].append(pss[h][CHUNK:2 * CHUNK] + rss[h][0:CHUNK])
                sts[h] = sts[h] * egl_row[h:h + 1, t0:t0 + 1] + rss[h][CHUNK:CHUNK + HEAD]

    for h in range(V_HEADS):
        state_ref[h] = sts[h]
        o = _rms(jnp.concatenate(outs[h], axis=0), ng_ref[...])
        zz = z_ref[:, h * HEAD:(h + 1) * HEAD].astype(F32)
        o_ref[:, h * HEAD:(h + 1) * HEAD] = (o * (zz * _sigmoid(zz))).astype(BF16)


def _gdn(p_main, p_ba, cw, alog, dtb, ng, *, batch, seq, ts, passes):
    t = p_main.shape[0]
    ns = seq // ts
    qkv_w = 2 * K_WIDTH + V_WIDTH
    row = lambda b, s: b * ns + s
    return pl.pallas_call(
        functools.partial(_gdn_kernel, ts=ts, passes=passes),
        grid=(batch, ns),
        in_specs=[
            pl.BlockSpec((ts, 2 * K_WIDTH), lambda b, s: (row(b, s), 2)),
            pl.BlockSpec((ts, V_WIDTH), lambda b, s: (row(b, s), 3)),
            pl.BlockSpec((ts, V_WIDTH), lambda b, s: (row(b, s), 4)),
            pl.BlockSpec((ts, BA_WIDTH), lambda b, s: (row(b, s), 0)),
            pl.BlockSpec((SHORT_K, qkv_w), lambda b, s: (0, 0)),
            pl.BlockSpec((1, LANES), lambda b, s: (0, 0)),
            pl.BlockSpec((1, LANES), lambda b, s: (0, 0)),
            pl.BlockSpec((1, HEAD), lambda b, s: (0, 0)),
        ],
        out_specs=pl.BlockSpec((ts, V_WIDTH), lambda b, s: (row(b, s), 0)),
        out_shape=jax.ShapeDtypeStruct((t, V_WIDTH), BF16),
        scratch_shapes=[
            pltpu.VMEM((SUBLANES + ts, qkv_w), F32),
            pltpu.VMEM((V_HEADS, HEAD, HEAD), F32),
        ],
        compiler_params=pltpu.CompilerParams(
            dimension_semantics=("parallel", "arbitrary"),
            vmem_limit_bytes=VMEM_LIMIT),
        name="gdn",
    )(p_main, p_main, p_main, p_ba, cw, alog, dtb, ng)


def _outproj_kernel(x_ref, oa_ref, ob_ref, wa_ref, wb_ref, o_ref):
    o_ref[...] = x_ref[...] + _dot(oa_ref[...], wa_ref[...]) + _dot(ob_ref[...], wb_ref[...])


def _outproj(x2d, oa, ob, wa, wb, *, tm):
    t, d = x2d.shape
    return pl.pallas_call(
        _outproj_kernel,
        grid=(t // tm,),
        in_specs=[
            pl.BlockSpec((tm, d), lambda i: (i, 0)),
            pl.BlockSpec((tm, CONV_CH), lambda i: (i, 0)),
            pl.BlockSpec((tm, V_WIDTH), lambda i: (i, 0)),
            pl.BlockSpec((CONV_CH, d), lambda i: (0, 0)),
            pl.BlockSpec((V_WIDTH, d), lambda i: (0, 0)),
        ],
        out_specs=pl.BlockSpec((tm, d), lambda i: (i, 0)),
        out_shape=jax.ShapeDtypeStruct((t, d), F32),
        compiler_params=pltpu.CompilerParams(
            dimension_semantics=("parallel",),
            vmem_limit_bytes=VMEM_LIMIT),
        name="outproj",
    )(x2d, oa, ob, wa, wb)


def _ffn_kernel(x_hbm, g_ref, wg_ref, wu_ref, cw_ref, cb_ref, wd_ref, fg_ref,
                o_ref, xbuf_ref, h_ref, act_ref, carry_ref, xsem, *, tm, ns, nrow, nf, final):
    s = pl.program_id(1)
    f = pl.program_id(2)
    row = pl.program_id(0) * ns + s

    def x_copy(r):
        return pltpu.make_async_copy(x_hbm.at[pl.ds(r * tm, tm), :], xbuf_ref, xsem)

    @pl.when(f == 0)
    def _():
        @pl.when(row == 0)
        def _():
            x_copy(row).start()

        x_copy(row).wait()
        x = xbuf_ref[...]
        h_ref[...] = _rms(x, g_ref[...]).astype(BF16)
        o_ref[...] = x
        act_ref[1] = jnp.zeros(act_ref.shape[1:], BF16)

        @pl.when(s == 0)
        def _():
            carry_ref[...] = jnp.zeros(carry_ref.shape, F32)

    @pl.when(jnp.logical_and(f == 1, row + 1 < nrow))
    def _():
        x_copy(row + 1).start()

    def step(slot):
        h = h_ref[...]
        gate = _dot(h, wg_ref[...])
        up = _dot(h, wu_ref[...])
        o_ref[...] += _dot(act_ref[1 - slot], wd_ref[...])
        g_all = jnp.concatenate([carry_ref[f], gate], axis=0)
        carry_ref[f] = gate[tm - SUBLANES:tm, :]
        gc = cw_ref[FFN_K - 1:FFN_K, :] * gate + cb_ref[...]
        for j in range(FFN_K - 1):
            delayed = pltpu.roll(g_all, FFN_K - 1 - j, axis=0)
            gc = gc + cw_ref[j:j + 1, :] * delayed[SUBLANES:SUBLANES + tm]
        act_ref[slot] = (gc * _sigmoid(gc) * up).astype(BF16)

    for slot in range(2):
        pl.when(jnp.logical_and(f < nf, f % 2 == slot))(functools.partial(step, slot))

    @pl.when(f == nf)
    def _():
        out = o_ref[...] + _dot(act_ref[(nf - 1) % 2], wd_ref[...])
        if final:
            out = _rms(out, fg_ref[...])
        o_ref[...] = out


def _ffn(x2d, g, w_up, cw, cb, w_down, fg, *, batch, seq, tm, tf, final):
    t, d = x2d.shape
    dff = w_down.shape[0]
    ns = seq // tm
    nf = dff // tf
    cur = lambda f: jnp.minimum(f, nf - 1)
    prev = lambda f: jnp.maximum(f - 1, 0)
    return pl.pallas_call(
        functools.partial(_ffn_kernel, tm=tm, ns=ns, nrow=batch * ns, nf=nf, final=final),
        grid=(batch, ns, nf + 1),
        in_specs=[
            pl.BlockSpec(memory_space=pl.ANY),
            pl.BlockSpec((1, d), lambda b, s, f: (0, 0)),
            pl.BlockSpec((d, tf), lambda b, s, f: (0, cur(f))),
            pl.BlockSpec((d, tf), lambda b, s, f: (0, nf + cur(f))),
            pl.BlockSpec((FFN_K, tf), lambda b, s, f: (0, cur(f))),
            pl.BlockSpec((1, tf), lambda b, s, f: (0, cur(f))),
            pl.BlockSpec((tf, d), lambda b, s, f: (prev(f), 0)),
            pl.BlockSpec((1, d), lambda b, s, f: (0, 0)),
        ],
        out_specs=pl.BlockSpec((tm, d), lambda b, s, f: (b * ns + s, 0)),
        out_shape=jax.ShapeDtypeStruct((t, d), F32),
        scratch_shapes=[
            pltpu.VMEM((tm, d), F32),
            pltpu.VMEM((tm, d), BF16),
            pltpu.VMEM((2, tm, tf), BF16),
            pltpu.VMEM((nf, SUBLANES, tf), F32),
            pltpu.SemaphoreType.DMA(()),
        ],
        compiler_params=pltpu.CompilerParams(
            dimension_semantics=("arbitrary", "arbitrary", "arbitrary"),
            vmem_limit_bytes=VMEM_LIMIT),
        name="ffn",
    )(x2d, g, w_up, w_up, cw, cb, w_down, fg)


def _pad_lanes(vec):
    return jnp.zeros((1, LANES), F32).at[0, 0:vec.shape[0]].set(vec)


def kernel(x, mix_norm_g, w_in, conv_dw_w, conv_dw_b, conv_ln_g, conv_ln_b, conv_pw_w, conv_pw_b,
           gdn_conv_w, gdn_a_log, gdn_dt_bias, gdn_norm_g, w_out, ffn_norm_g, w_up,
           ffn_conv_w, ffn_conv_b, w_down, final_norm_g):
    batch, seq, d = x.shape
    depth = w_in.shape[0]
    t = batch * seq
    x2d = x.reshape(t, d)
    fg = final_norm_g.reshape(1, d)
    b_off = MAIN_WIDTH
    a_off = MAIN_WIDTH + V_HEADS

    for l in range(depth):
        w_main = w_in[l, :, 0:MAIN_WIDTH].astype(BF16)
        w_ba = jnp.zeros((d, BA_WIDTH), F32)
        w_ba = w_ba.at[:, 0:V_HEADS].set(w_in[l, :, b_off:b_off + V_HEADS])
        w_ba = w_ba.at[:, LANES:LANES + V_HEADS].set(w_in[l, :, a_off:a_off + V_HEADS])
        p_main, p_ba = _inproj(x2d, mix_norm_g[l].reshape(1, d), w_main, w_ba.astype(BF16),
                               tm=min(1024, seq), tn=1024)
        out_a = _conformer(p_main, conv_dw_w[l], conv_dw_b[l].reshape(1, -1),
                           conv_ln_g[l].reshape(1, -1), conv_ln_b[l].reshape(1, -1),
                           conv_pw_w[l].astype(BF16), conv_pw_b[l].reshape(1, -1),
                           batch=batch, seq=seq, ts=min(512, seq))
        out_b = _gdn(p_main, p_ba, gdn_conv_w[l], _pad_lanes(gdn_a_log[l]),
                     _pad_lanes(gdn_dt_bias[l]), gdn_norm_g[l].reshape(1, -1),
                     batch=batch, seq=seq, ts=256, passes=1)
        wo = w_out[l].astype(BF16)
        x2d = _outproj(x2d, out_a, out_b, wo[0:CONV_CH], wo[CONV_CH:], tm=min(512, seq))
        x2d = _ffn(x2d, ffn_norm_g[l].reshape(1, d), w_up[l].astype(BF16), ffn_conv_w[l],
                   ffn_conv_b[l].reshape(1, -1), w_down[l].astype(BF16), fg,
                   batch=batch, seq=seq, tm=min(1024, seq), tf=512, final=(l == depth - 1))
    return x2d.reshape(batch, seq, d)
```

```python
import functools

import jax
import jax.numpy as jnp
from jax import lax
from jax.experimental import pallas as pl
from jax.experimental.pallas import tpu as pltpu

F32 = jnp.float32
BF16 = jnp.bfloat16
EPS = 1e-6

LANES = 128
SUBLANES = 8
CONV_CH = 1024
CONV_GROUPS = 8
CONV_K = 31
HEAD = 128
V_HEADS = 8
QK_HEADS = 4
K_WIDTH = QK_HEADS * HEAD
V_WIDTH = V_HEADS * HEAD
SHORT_K = 4
CHUNK = 64
GROUP = 128
FFN_K = 3
MAIN_WIDTH = 2 * CONV_CH + 2 * K_WIDTH + 2 * V_WIDTH
BA_WIDTH = 2 * LANES
CONV_ROWS = 64
CONV_HIST = 32
VMEM_LIMIT = 56 * 1024 * 1024


def _dot(a, b):
    return jnp.dot(a, b, preferred_element_type=F32)


def _rms(x, g):
    return x * lax.rsqrt(jnp.mean(x * x, axis=-1, keepdims=True) + EPS) * g


def _sigmoid(x):
    return 1.0 / (1.0 + jnp.exp(-x))


def _split3(x):
    x1 = x.astype(BF16)
    r1 = x - x1.astype(F32)
    x2 = r1.astype(BF16)
    r2 = r1 - x2.astype(F32)
    return x1, x2, r2.astype(BF16)


def _dot_sel(sel_bf16, x):
    x1, x2, x3 = _split3(x)
    return _dot(sel_bf16, x1) + _dot(sel_bf16, x2) + _dot(sel_bf16, x3)


def _mm(a, b, passes):
    if passes == 1:
        return _dot(a.astype(BF16), b.astype(BF16))
    ah = a.astype(BF16)
    al = (a - ah.astype(F32)).astype(BF16)
    bh = b.astype(BF16)
    bl = (b - bh.astype(F32)).astype(BF16)
    return _dot(ah, bh) + _dot(ah, bl) + _dot(al, bh)


def _inproj_kernel(x_ref, g_ref, w_ref, wba_ref, p_ref, ba_ref, h_ref):
    @pl.when(pl.program_id(1) == 0)
    def _():
        h = _rms(x_ref[...], g_ref[...]).astype(BF16)
        h_ref[...] = h
        ba_ref[...] = _dot(h, wba_ref[...])

    p_ref[...] = _dot(h_ref[...], w_ref[...]).astype(BF16)


def _inproj(x2d, g, w_main, w_ba, *, tm, tn):
    t, d = x2d.shape
    n = w_main.shape[1]
    return pl.pallas_call(
        _inproj_kernel,
        grid=(t // tm, n // tn),
        in_specs=[
            pl.BlockSpec((tm, d), lambda i, j: (i, 0)),
            pl.BlockSpec((1, d), lambda i, j: (0, 0)),
            pl.BlockSpec((d, tn), lambda i, j: (0, j)),
            pl.BlockSpec((d, BA_WIDTH), lambda i, j: (0, 0)),
        ],
        out_specs=[
            pl.BlockSpec((tm, tn), lambda i, j: (i, j)),
            pl.BlockSpec((tm, BA_WIDTH), lambda i, j: (i, 0)),
        ],
        out_shape=[
            jax.ShapeDtypeStruct((t, n), BF16),
            jax.ShapeDtypeStruct((t, BA_WIDTH), F32),
        ],
        scratch_shapes=[pltpu.VMEM((tm, d), BF16)],
        compiler_params=pltpu.CompilerParams(
            dimension_semantics=("parallel", "arbitrary"),
            vmem_limit_bytes=VMEM_LIMIT),
        name="inproj",
    )(x2d, g, w_main, w_ba)


def _conf_kernel(av_ref, ag_ref, dww_ref, dwb_ref, lng_ref, lnb_ref, pww_ref, pwb_ref,
                 o_ref, ush_ref, y_ref, *, ts):
    rows = CONV_HIST + ts

    @pl.when(pl.program_id(1) == 0)
    def _():
        ush_ref[0, 0:CONV_HIST, :] = jnp.zeros((CONV_HIST, CONV_CH), F32)

    ush_ref[0, CONV_HIST:rows, :] = av_ref[...].astype(F32) * _sigmoid(ag_ref[...].astype(F32))
    u_all = ush_ref[0]
    for r in range(1, SUBLANES):
        ush_ref[r, SUBLANES:rows, :] = pltpu.roll(u_all, r, axis=0)[SUBLANES:rows]

    cg = CONV_CH // CONV_GROUPS
    for rb in range(ts // CONV_ROWS):
        for gi in range(CONV_GROUPS):
            lanes = slice(gi * cg, (gi + 1) * cg)
            acc = None
            for j in range(CONV_K):
                a, r = divmod(CONV_K - 1 - j, SUBLANES)
                start = CONV_HIST - a * SUBLANES + rb * CONV_ROWS
                term = dww_ref[j:j + 1, lanes] * ush_ref[r, start:start + CONV_ROWS, lanes]
                acc = term if acc is None else acc + term
            acc = acc + dwb_ref[:, lanes]
            xc = acc - jnp.mean(acc, axis=-1, keepdims=True)
            var = jnp.mean(xc * xc, axis=-1, keepdims=True)
            yv = xc * lax.rsqrt(var + EPS) * lng_ref[:, lanes] + lnb_ref[:, lanes]
            y_ref[rb * CONV_ROWS:(rb + 1) * CONV_ROWS, lanes] = (yv * _sigmoid(yv)).astype(BF16)

    ush_ref[0, 0:CONV_HIST, :] = ush_ref[0, ts:rows, :]
    o_ref[...] = (_dot(y_ref[...], pww_ref[...]) + pwb_ref[...]).astype(BF16)


def _conformer(p_main, dww, dwb, lng, lnb, pww, pwb, *, batch, seq, ts):
    t = p_main.shape[0]
    ns = seq // ts
    vec = lambda: pl.BlockSpec((1, CONV_CH), lambda b, s: (0, 0))
    return pl.pallas_call(
        functools.partial(_conf_kernel, ts=ts),
        grid=(batch, ns),
        in_specs=[
            pl.BlockSpec((ts, CONV_CH), lambda b, s: (b * ns + s, 0)),
            pl.BlockSpec((ts, CONV_CH), lambda b, s: (b * ns + s, 1)),
            pl.BlockSpec((CONV_K, CONV_CH), lambda b, s: (0, 0)),
            vec(), vec(), vec(),
            pl.BlockSpec((CONV_CH, CONV_CH), lambda b, s: (0, 0)),
            vec(),
        ],
        out_specs=pl.BlockSpec((ts, CONV_CH), lambda b, s: (b * ns + s, 0)),
        out_shape=jax.ShapeDtypeStruct((t, CONV_CH), BF16),
        scratch_shapes=[
            pltpu.VMEM((SUBLANES, CONV_HIST + ts, CONV_CH), F32),
            pltpu.VMEM((ts, CONV_CH), BF16),
        ],
        compiler_params=pltpu.CompilerParams(
            dimension_semantics=("parallel", "arbitrary"),
            vmem_limit_bytes=VMEM_LIMIT),
        name="conformer",
    )(p_main, p_main, dww, dwb, lng, lnb, pww, pwb)


def _gdn_kernel(qk_ref, v_ref, z_ref, ba_ref, cw_ref, alog_ref, dtb_ref, ng_ref,
                o_ref, xbuf_ref, state_ref, *, ts, passes):
    qkv_w = 2 * K_WIDTH + V_WIDTH

    @pl.when(pl.program_id(1) == 0)
    def _():
        xbuf_ref[0:SUBLANES, :] = jnp.zeros((SUBLANES, qkv_w), F32)
        state_ref[...] = jnp.zeros(state_ref.shape, F32)

    xbuf_ref[SUBLANES:SUBLANES + ts, 0:2 * K_WIDTH] = qk_ref[...].astype(F32)
    xbuf_ref[SUBLANES:SUBLANES + ts, 2 * K_WIDTH:qkv_w] = v_ref[...].astype(F32)
    x_all = xbuf_ref[...]
    y = cw_ref[SHORT_K - 1:SHORT_K, :] * x_all[SUBLANES:SUBLANES + ts]
    for j in range(SHORT_K - 1):
        delayed = pltpu.roll(x_all, SHORT_K - 1 - j, axis=0)
        y = y + cw_ref[j:j + 1, :] * delayed[SUBLANES:SUBLANES + ts]
    xbuf_ref[0:SUBLANES, :] = x_all[ts:ts + SUBLANES]
    y = y * _sigmoid(y)

    beta = _sigmoid(ba_ref[:, 0:LANES])
    araw = ba_ref[:, LANES:2 * LANES]
    g = -jnp.exp(alog_ref[...]) * jax.nn.softplus(araw + dtb_ref[...])

    ri = lax.broadcasted_iota(jnp.int32, (ts, ts), 0)
    ci = lax.broadcasted_iota(jnp.int32, (ts, ts), 1)
    shift = CHUNK.bit_length() - 1
    same_t = (ri >> shift) == (ci >> shift)
    sel = jnp.concatenate([jnp.where(jnp.logical_and(same_t, ri >= ci), 1.0, 0.0),
                           jnp.where(same_t, 1.0, 0.0)], axis=0).astype(BF16)
    gsum = _dot_sel(sel, g)
    gc_col = gsum[0:ts]
    gl_col = gsum[ts:2 * ts]
    gc_row = gc_col.T
    gl_row = gl_col.T
    eg_col = jnp.exp(gc_col)
    kd_row = jnp.exp(gl_row - gc_row)
    egl_row = jnp.exp(gl_row)

    rg = lax.broadcasted_iota(jnp.int32, (GROUP, GROUP), 0)
    cg = lax.broadcasted_iota(jnp.int32, (GROUP, GROUP), 1)
    same = (rg >> shift) == (cg >> shift)
    causal = jnp.logical_and(same, rg >= cg)
    strict = jnp.logical_and(same, rg > cg)
    eye = jnp.where(rg == cg, 1.0, 0.0).astype(F32)

    scale = HEAD ** -0.5
    kts = []
    ks = []
    qs = []
    for kh in range(QK_HEADS):
        q = y[:, kh * HEAD:(kh + 1) * HEAD]
        k = y[:, K_WIDTH + kh * HEAD:K_WIDTH + (kh + 1) * HEAD]
        q = q * lax.rsqrt(jnp.sum(q * q, axis=-1, keepdims=True) + EPS)
        k = k * lax.rsqrt(jnp.sum(k * k, axis=-1, keepdims=True) + EPS)
        qs.append(q * scale)
        ks.append(k)
        kts.append(k.T)

    rep = V_HEADS // QK_HEADS
    ngr = ts // GROUP
    units = [(h, gi) for h in range(V_HEADS) for gi in range(ngr)]
    kt16s = [kt.astype(BF16) for kt in kts]
    lows, qk16s, rhss, qgs, kdts = {}, {}, {}, {}, {}
    for h, gi in units:
        rows = slice(gi * GROUP, (gi + 1) * GROUP)
        q, k, kt16 = qs[h // rep][rows], ks[h // rep][rows], kt16s[h // rep][:, rows]
        v = y[rows, 2 * K_WIDTH + h * HEAD:2 * K_WIDTH + (h + 1) * HEAD]
        bcol = beta[rows, h:h + 1]
        egc = eg_col[rows, h:h + 1]
        decay = jnp.exp(jnp.where(causal, gc_col[rows, h:h + 1] - gc_row[h:h + 1, rows], -jnp.inf))
        kb = k * bcol
        lows[h, gi] = jnp.where(strict, _dot(kb.astype(BF16), kt16) * decay, 0.0)
        qk16s[h, gi] = (_dot(q.astype(BF16), kt16) * decay).astype(BF16)
        rhss[h, gi] = jnp.concatenate([v * bcol, kb * egc], axis=1)
        qgs[h, gi] = q * egc
        kdts[h, gi] = (kts[h // rep][:, rows] * kd_row[h:h + 1, rows]).astype(BF16)

    invs = {u: eye - lows[u] for u in units}
    powers = {u: _mm(lows[u], lows[u], passes) for u in units}
    span = 4
    while span < CHUNK:
        prods = {u: _mm(jnp.concatenate([powers[u], invs[u]], axis=0), powers[u], passes) for u in units}
        powers = {u: prods[u][0:GROUP] for u in units}
        invs = {u: invs[u] + prods[u][GROUP:2 * GROUP] for u in units}
        span *= 2
    invs = {u: invs[u] + _mm(invs[u], powers[u], passes) for u in units}
    uws = {u: _mm(invs[u], rhss[u], passes) for u in units}

    sts = [state_ref[h] for h in range(V_HEADS)]
    outs = [[] for _ in range(V_HEADS)]
    for gi in range(ngr):
        for c in range(GROUP // CHUNK):
            r0, r1 = c * CHUNK, (c + 1) * CHUNK
            t0 = gi * GROUP + r0
            pss = []
            for h in range(V_HEADS):
                lhs = jnp.concatenate([uws[h, gi][r0:r1, HEAD:2 * HEAD], qgs[h, gi][r0:r1]], axis=0)
                pss.append(_dot(lhs.astype(BF16), sts[h].astype(BF16)))
            rss = []
            for h in range(V_HEADS):
                vnew = uws[h, gi][r0:r1, 0:HEAD] - pss[h][0:CHUNK]
                pieces = []
                if r0 > 0:
                    pieces.append(jnp.zeros((r0, HEAD), F32))
                pieces.append(vnew)
                if r1 < GROUP:
                    pieces.append(jnp.zeros((GROUP - r1, HEAD), F32))
                vfull = jnp.concatenate(pieces, axis=0).astype(BF16)
                rss.append(_dot(jnp.concatenate([qk16s[h, gi][r0:r1], kdts[h, gi]], axis=0), vfull))
            for h in range(V_HEADS):
                outs[h].append(pss[h][CHUNK:2 * CHUNK] + rss[h][0:CHUNK])
                sts[h] = sts[h] * egl_row[h:h + 1, t0:t0 + 1] + rss[h][CHUNK:CHUNK + HEAD]

    for h in range(V_HEADS):
        state_ref[h] = sts[h]
        o = _rms(jnp.concatenate(outs[h], axis=0), ng_ref[...])
        zz = z_ref[:, h * HEAD:(h + 1) * HEAD].astype(F32)
        o_ref[:, h * HEAD:(h + 1) * HEAD] = (o * (zz * _sigmoid(zz))).astype(BF16)


def _gdn(p_main, p_ba, cw, alog, dtb, ng, *, batch, seq, ts, passes):
    t = p_main.shape[0]
    ns = seq // ts
    qkv_w = 2 * K_WIDTH + V_WIDTH
    row = lambda b, s: b * ns + s
    return pl.pallas_call(
        functools.partial(_gdn_kernel, ts=ts, passes=passes),
        grid=(batch, ns),
        in_specs=[
            pl.BlockSpec((ts, 2 * K_WIDTH), lambda b, s: (row(b, s), 2)),
            pl.BlockSpec((ts, V_WIDTH), lambda b, s: (row(b, s), 3)),
            pl.BlockSpec((ts, V_WIDTH), lambda b, s: (row(b, s), 4)),
            pl.BlockSpec((ts, BA_WIDTH), lambda b, s: (row(b, s), 0)),
            pl.BlockSpec((SHORT_K, qkv_w), lambda b, s: (0, 0)),
            pl.BlockSpec((1, LANES), lambda b, s: (0, 0)),
            pl.BlockSpec((1, LANES), lambda b, s: (0, 0)),
            pl.BlockSpec((1, HEAD), lambda b, s: (0, 0)),
        ],
        out_specs=pl.BlockSpec((ts, V_WIDTH), lambda b, s: (row(b, s), 0)),
        out_shape=jax.ShapeDtypeStruct((t, V_WIDTH), BF16),
        scratch_shapes=[
            pltpu.VMEM((SUBLANES + ts, qkv_w), F32),
            pltpu.VMEM((V_HEADS, HEAD, HEAD), F32),
        ],
        compiler_params=pltpu.CompilerParams(
            dimension_semantics=("parallel", "arbitrary"),
            vmem_limit_bytes=VMEM_LIMIT),
        name="gdn",
    )(p_main, p_main, p_main, p_ba, cw, alog, dtb, ng)


def _outproj_kernel(x_ref, oa_ref, ob_ref, wa_ref, wb_ref, o_ref):
    o_ref[...] = x_ref[...] + _dot(oa_ref[...], wa_ref[...]) + _dot(ob_ref[...], wb_ref[...])


def _outproj(x2d, oa, ob, wa, wb, *, tm):
    t, d = x2d.shape
    return pl.pallas_call(
        _outproj_kernel,
        grid=(t // tm,),
        in_specs=[
            pl.BlockSpec((tm, d), lambda i: (i, 0)),
            pl.BlockSpec((tm, CONV_CH), lambda i: (i, 0)),
            pl.BlockSpec((tm, V_WIDTH), lambda i: (i, 0)),
            pl.BlockSpec((CONV_CH, d), lambda i: (0, 0)),
            pl.BlockSpec((V_WIDTH, d), lambda i: (0, 0)),
        ],
        out_specs=pl.BlockSpec((tm, d), lambda i: (i, 0)),
        out_shape=jax.ShapeDtypeStruct((t, d), F32),
        compiler_params=pltpu.CompilerParams(
            dimension_semantics=("parallel",),
            vmem_limit_bytes=VMEM_LIMIT),
        name="outproj",
    )(x2d, oa, ob, wa, wb)


def _ffn_kernel(x_hbm, g_ref, wg_ref, wu_ref, cw_ref, cb_ref, wd_ref, fg_ref,
                o_ref, xbuf_ref, h_ref, act_ref, carry_ref, xsem, *, tm, ns, nrow, nf, final):
    s = pl.program_id(1)
    f = pl.program_id(2)
    row = pl.program_id(0) * ns + s

    def x_copy(r):
        return pltpu.make_async_copy(x_hbm.at[pl.ds(r * tm, tm), :], xbuf_ref, xsem)

    @pl.when(f == 0)
    def _():
        @pl.when(row == 0)
        def _():
            x_copy(row).start()

        x_copy(row).wait()
        x = xbuf_ref[...]
        h_ref[...] = _rms(x, g_ref[...]).astype(BF16)
        o_ref[...] = x
        act_ref[1] = jnp.zeros(act_ref.shape[1:], BF16)

        @pl.when(s == 0)
        def _():
            carry_ref[...] = jnp.zeros(carry_ref.shape, F32)

    @pl.when(jnp.logical_and(f == 1, row + 1 < nrow))
    def _():
        x_copy(row + 1).start()

    def step(slot):
        h = h_ref[...]
        gate = _dot(h, wg_ref[...])
        up = _dot(h, wu_ref[...])
        o_ref[...] += _dot(act_ref[1 - slot], wd_ref[...])
        g_all = jnp.concatenate([carry_ref[f], gate], axis=0)
        carry_ref[f] = gate[tm - SUBLANES:tm, :]
        gc = cw_ref[FFN_K - 1:FFN_K, :] * gate + cb_ref[...]
        for j in range(FFN_K - 1):
            delayed = pltpu.roll(g_all, FFN_K - 1 - j, axis=0)
            gc = gc + cw_ref[j:j + 1, :] * delayed[SUBLANES:SUBLANES + tm]
        act_ref[slot] = (gc * _sigmoid(gc) * up).astype(BF16)

    for slot in range(2):
        pl.when(jnp.logical_and(f < nf, f % 2 == slot))(functools.partial(step, slot))

    @pl.when(f == nf)
    def _():
        out = o_ref[...] + _dot(act_ref[(nf - 1) % 2], wd_ref[...])
        if final:
            out = _rms(out, fg_ref[...])
        o_ref[...] = out


def _ffn(x2d, g, w_up, cw, cb, w_down, fg, *, batch, seq, tm, tf, final):
    t, d = x2d.shape
    dff = w_down.shape[0]
    ns = seq // tm
    nf = dff // tf
    cur = lambda f: jnp.minimum(f, nf - 1)
    prev = lambda f: jnp.maximum(f - 1, 0)
    return pl.pallas_call(
        functools.partial(_ffn_kernel, tm=tm, ns=ns, nrow=batch * ns, nf=nf, final=final),
        grid=(batch, ns, nf + 1),
        in_specs=[
            pl.BlockSpec(memory_space=pl.ANY),
            pl.BlockSpec((1, d), lambda b, s, f: (0, 0)),
            pl.BlockSpec((d, tf), lambda b, s, f: (0, cur(f))),
            pl.BlockSpec((d, tf), lambda b, s, f: (0, nf + cur(f))),
            pl.BlockSpec((FFN_K, tf), lambda b, s, f: (0, cur(f))),
            pl.BlockSpec((1, tf), lambda b, s, f: (0, cur(f))),
            pl.BlockSpec((tf, d), lambda b, s, f: (prev(f), 0)),
            pl.BlockSpec((1, d), lambda b, s, f: (0, 0)),
        ],
        out_specs=pl.BlockSpec((tm, d), lambda b, s, f: (b * ns + s, 0)),
        out_shape=jax.ShapeDtypeStruct((t, d), F32),
        scratch_shapes=[
            pltpu.VMEM((tm, d), F32),
            pltpu.VMEM((tm, d), BF16),
            pltpu.VMEM((2, tm, tf), BF16),
            pltpu.VMEM((nf, SUBLANES, tf), F32),
            pltpu.SemaphoreType.DMA(()),
        ],
        compiler_params=pltpu.CompilerParams(
            dimension_semantics=("arbitrary", "arbitrary", "arbitrary"),
            vmem_limit_bytes=VMEM_LIMIT),
        name="ffn",
    )(x2d, g, w_up, w_up, cw, cb, w_down, fg)


def _pad_lanes(vec):
    return jnp.zeros((1, LANES), F32).at[0, 0:vec.shape[0]].set(vec)


def kernel(x, mix_norm_g, w_in, conv_dw_w, conv_dw_b, conv_ln_g, conv_ln_b, conv_pw_w, conv_pw_b,
           gdn_conv_w, gdn_a_log, gdn_dt_bias, gdn_norm_g, w_out, ffn_norm_g, w_up,
           ffn_conv_w, ffn_conv_b, w_down, final_norm_g):
    batch, seq, d = x.shape
    depth = w_in.shape[0]
    t = batch * seq
    x2d = x.reshape(t, d)
    fg = final_norm_g.reshape(1, d)
    b_off = MAIN_WIDTH
    a_off = MAIN_WIDTH + V_HEADS

    for l in range(depth):
        w_main = w_in[l, :, 0:MAIN_WIDTH].astype(BF16)
        w_ba = jnp.zeros((d, BA_WIDTH), F32)
        w_ba = w_ba.at[:, 0:V_HEADS].set(w_in[l, :, b_off:b_off + V_HEADS])
        w_ba = w_ba.at[:, LANES:LANES + V_HEADS].set(w_in[l, :, a_off:a_off + V_HEADS])
        p_main, p_ba = _inproj(x2d, mix_norm_g[l].reshape(1, d), w_main, w_ba.astype(BF16),
                               tm=min(1024, seq), tn=1024)
        out_a = _conformer(p_main, conv_dw_w[l], conv_dw_b[l].reshape(1, -1),
                           conv_ln_g[l].reshape(1, -1), conv_ln_b[l].reshape(1, -1),
                           conv_pw_w[l].astype(BF16), conv_pw_b[l].reshape(1, -1),
                           batch=batch, seq=seq, ts=min(512, seq))
        out_b = _gdn(p_main, p_ba, gdn_conv_w[l], _pad_lanes(gdn_a_log[l]),
                     _pad_lanes(gdn_dt_bias[l]), gdn_norm_g[l].reshape(1, -1),
                     batch=batch, seq=seq, ts=min(512, seq), passes=1)
        wo = w_out[l].astype(BF16)
        x2d = _outproj(x2d, out_a, out_b, wo[0:CONV_CH], wo[CONV_CH:], tm=min(512, seq))
        x2d = _ffn(x2d, ffn_norm_g[l].reshape(1, d), w_up[l].astype(BF16), ffn_conv_w[l],
                   ffn_conv_b[l].reshape(1, -1), w_down[l].astype(BF16), fg,
                   batch=batch, seq=seq, tm=min(1024, seq), tf=512, final=(l == depth - 1))
    return x2d.reshape(batch, seq, d)
```

```python
import functools

import jax
import jax.numpy as jnp
from jax import lax
from jax.experimental import pallas as pl
from jax.experimental.pallas import tpu as pltpu

F32 = jnp.float32
BF16 = jnp.bfloat16
EPS = 1e-6

LANES = 128
SUBLANES = 8
CONV_CH = 1024
CONV_GROUPS = 8
CONV_K = 31
HEAD = 128
V_HEADS = 8
QK_HEADS = 4
K_WIDTH = QK_HEADS * HEAD
V_WIDTH = V_HEADS * HEAD
SHORT_K = 4
CHUNK = 64
GROUP = 128
FFN_K = 3
MAIN_WIDTH = 2 * CONV_CH + 2 * K_WIDTH + 2 * V_WIDTH
BA_WIDTH = 2 * LANES
CONV_ROWS = 64
CONV_HIST = 32
VMEM_LIMIT = 56 * 1024 * 1024


def _dot(a, b):
    return jnp.dot(a, b, preferred_element_type=F32)


def _rms(x, g):
    return x * lax.rsqrt(jnp.mean(x * x, axis=-1, keepdims=True) + EPS) * g


def _sigmoid(x):
    return 1.0 / (1.0 + jnp.exp(-x))


def _split3(x):
    x1 = x.astype(BF16)
    r1 = x - x1.astype(F32)
    x2 = r1.astype(BF16)
    r2 = r1 - x2.astype(F32)
    return x1, x2, r2.astype(BF16)


def _dot_sel(sel_bf16, x):
    x1, x2, x3 = _split3(x)
    return _dot(sel_bf16, x1) + _dot(sel_bf16, x2) + _dot(sel_bf16, x3)


def _mm(a, b, passes):
    if passes == 1:
        return _dot(a.astype(BF16), b.astype(BF16))
    ah = a.astype(BF16)
    al = (a - ah.astype(F32)).astype(BF16)
    bh = b.astype(BF16)
    bl = (b - bh.astype(F32)).astype(BF16)
    return _dot(ah, bh) + _dot(ah, bl) + _dot(al, bh)


def _inproj_kernel(x_ref, g_ref, w_ref, wba_ref, p_ref, ba_ref, h_ref):
    @pl.when(pl.program_id(1) == 0)
    def _():
        h = _rms(x_ref[...], g_ref[...]).astype(BF16)
        h_ref[...] = h
        ba_ref[...] = _dot(h, wba_ref[...])

    p_ref[...] = _dot(h_ref[...], w_ref[...]).astype(BF16)


def _inproj(x2d, g, w_main, w_ba, *, tm, tn):
    t, d = x2d.shape
    n = w_main.shape[1]
    return pl.pallas_call(
        _inproj_kernel,
        grid=(t // tm, n // tn),
        in_specs=[
            pl.BlockSpec((tm, d), lambda i, j: (i, 0)),
            pl.BlockSpec((1, d), lambda i, j: (0, 0)),
            pl.BlockSpec((d, tn), lambda i, j: (0, j)),
            pl.BlockSpec((d, BA_WIDTH), lambda i, j: (0, 0)),
        ],
        out_specs=[
            pl.BlockSpec((tm, tn), lambda i, j: (i, j)),
            pl.BlockSpec((tm, BA_WIDTH), lambda i, j: (i, 0)),
        ],
        out_shape=[
            jax.ShapeDtypeStruct((t, n), BF16),
            jax.ShapeDtypeStruct((t, BA_WIDTH), F32),
        ],
        scratch_shapes=[pltpu.VMEM((tm, d), BF16)],
        compiler_params=pltpu.CompilerParams(
            dimension_semantics=("parallel", "arbitrary"),
            vmem_limit_bytes=VMEM_LIMIT),
        name="inproj",
    )(x2d, g, w_main, w_ba)


def _conf_kernel(av_ref, ag_ref, dww_ref, dwb_ref, lng_ref, lnb_ref, pww_ref, pwb_ref,
                 o_ref, ush_ref, y_ref, *, ts):
    rows = CONV_HIST + ts

    @pl.when(pl.program_id(1) == 0)
    def _():
        ush_ref[0, 0:CONV_HIST, :] = jnp.zeros((CONV_HIST, CONV_CH), F32)

    ush_ref[0, CONV_HIST:rows, :] = av_ref[...].astype(F32) * _sigmoid(ag_ref[...].astype(F32))
    u_all = ush_ref[0]
    for r in range(1, SUBLANES):
        ush_ref[r, SUBLANES:rows, :] = pltpu.roll(u_all, r, axis=0)[SUBLANES:rows]

    cg = CONV_CH // CONV_GROUPS
    for rb in range(ts // CONV_ROWS):
        for gi in range(CONV_GROUPS):
            lanes = slice(gi * cg, (gi + 1) * cg)
            acc = None
            for j in range(CONV_K):
                a, r = divmod(CONV_K - 1 - j, SUBLANES)
                start = CONV_HIST - a * SUBLANES + rb * CONV_ROWS
                term = dww_ref[j:j + 1, lanes] * ush_ref[r, start:start + CONV_ROWS, lanes]
                acc = term if acc is None else acc + term
            acc = acc + dwb_ref[:, lanes]
            xc = acc - jnp.mean(acc, axis=-1, keepdims=True)
            var = jnp.mean(xc * xc, axis=-1, keepdims=True)
            yv = xc * lax.rsqrt(var + EPS) * lng_ref[:, lanes] + lnb_ref[:, lanes]
            y_ref[rb * CONV_ROWS:(rb + 1) * CONV_ROWS, lanes] = (yv * _sigmoid(yv)).astype(BF16)

    ush_ref[0, 0:CONV_HIST, :] = ush_ref[0, ts:rows, :]
    o_ref[...] = (_dot(y_ref[...], pww_ref[...]) + pwb_ref[...]).astype(BF16)


def _conformer(p_main, dww, dwb, lng, lnb, pww, pwb, *, batch, seq, ts):
    t = p_main.shape[0]
    ns = seq // ts
    vec = lambda: pl.BlockSpec((1, CONV_CH), lambda b, s: (0, 0))
    return pl.pallas_call(
        functools.partial(_conf_kernel, ts=ts),
        grid=(batch, ns),
        in_specs=[
            pl.BlockSpec((ts, CONV_CH), lambda b, s: (b * ns + s, 0)),
            pl.BlockSpec((ts, CONV_CH), lambda b, s: (b * ns + s, 1)),
            pl.BlockSpec((CONV_K, CONV_CH), lambda b, s: (0, 0)),
            vec(), vec(), vec(),
            pl.BlockSpec((CONV_CH, CONV_CH), lambda b, s: (0, 0)),
            vec(),
        ],
        out_specs=pl.BlockSpec((ts, CONV_CH), lambda b, s: (b * ns + s, 0)),
        out_shape=jax.ShapeDtypeStruct((t, CONV_CH), BF16),
        scratch_shapes=[
            pltpu.VMEM((SUBLANES, CONV_HIST + ts, CONV_CH), F32),
            pltpu.VMEM((ts, CONV_CH), BF16),
        ],
        compiler_params=pltpu.CompilerParams(
            dimension_semantics=("parallel", "arbitrary"),
            vmem_limit_bytes=VMEM_LIMIT),
        name="conformer",
    )(p_main, p_main, dww, dwb, lng, lnb, pww, pwb)


def _gdn_kernel(qk_ref, v_ref, z_ref, ba_ref, cw_ref, alog_ref, dtb_ref, ng_ref,
                o_ref, xbuf_ref, state_ref, *, ts, passes):
    qkv_w = 2 * K_WIDTH + V_WIDTH

    @pl.when(pl.program_id(1) == 0)
    def _():
        xbuf_ref[0:SUBLANES, :] = jnp.zeros((SUBLANES, qkv_w), F32)
        state_ref[...] = jnp.zeros(state_ref.shape, F32)

    xbuf_ref[SUBLANES:SUBLANES + ts, 0:2 * K_WIDTH] = qk_ref[...].astype(F32)
    xbuf_ref[SUBLANES:SUBLANES + ts, 2 * K_WIDTH:qkv_w] = v_ref[...].astype(F32)
    x_all = xbuf_ref[...]
    y = cw_ref[SHORT_K - 1:SHORT_K, :] * x_all[SUBLANES:SUBLANES + ts]
    for j in range(SHORT_K - 1):
        delayed = pltpu.roll(x_all, SHORT_K - 1 - j, axis=0)
        y = y + cw_ref[j:j + 1, :] * delayed[SUBLANES:SUBLANES + ts]
    xbuf_ref[0:SUBLANES, :] = x_all[ts:ts + SUBLANES]
    y = y * _sigmoid(y)

    beta = _sigmoid(ba_ref[:, 0:LANES])
    araw = ba_ref[:, LANES:2 * LANES]
    g = -jnp.exp(alog_ref[...]) * jax.nn.softplus(araw + dtb_ref[...])

    ri = lax.broadcasted_iota(jnp.int32, (ts, ts), 0)
    ci = lax.broadcasted_iota(jnp.int32, (ts, ts), 1)
    shift = CHUNK.bit_length() - 1
    same_t = (ri >> shift) == (ci >> shift)
    sel = jnp.concatenate([jnp.where(jnp.logical_and(same_t, ri >= ci), 1.0, 0.0),
                           jnp.where(same_t, 1.0, 0.0)], axis=0).astype(BF16)
    gsum = _dot_sel(sel, g)
    gc_col = gsum[0:ts]
    gl_col = gsum[ts:2 * ts]
    gc_row = gc_col.T
    gl_row = gl_col.T
    eg_col = jnp.exp(gc_col)
    kd_row = jnp.exp(gl_row - gc_row)
    egl_row = jnp.exp(gl_row)

    rg = lax.broadcasted_iota(jnp.int32, (GROUP, GROUP), 0)
    cg = lax.broadcasted_iota(jnp.int32, (GROUP, GROUP), 1)
    same = (rg >> shift) == (cg >> shift)
    causal = jnp.logical_and(same, rg >= cg)
    strict = jnp.logical_and(same, rg > cg)
    eye = jnp.where(rg == cg, 1.0, 0.0).astype(F32)

    scale = HEAD ** -0.5
    kts = []
    ks = []
    qs = []
    for kh in range(QK_HEADS):
        q = y[:, kh * HEAD:(kh + 1) * HEAD]
        k = y[:, K_WIDTH + kh * HEAD:K_WIDTH + (kh + 1) * HEAD]
        q = q * lax.rsqrt(jnp.sum(q * q, axis=-1, keepdims=True) + EPS)
        k = k * lax.rsqrt(jnp.sum(k * k, axis=-1, keepdims=True) + EPS)
        qs.append(q * scale)
        ks.append(k)
        kts.append(k.T)

    rep = V_HEADS // QK_HEADS
    ngr = ts // GROUP
    units = [(h, gi) for h in range(V_HEADS) for gi in range(ngr)]
    kt16s = [kt.astype(BF16) for kt in kts]
    lows, qk16s, rhss, qgs, kdts = {}, {}, {}, {}, {}
    for h, gi in units:
        rows = slice(gi * GROUP, (gi + 1) * GROUP)
        q, k, kt16 = qs[h // rep][rows], ks[h // rep][rows], kt16s[h // rep][:, rows]
        v = y[rows, 2 * K_WIDTH + h * HEAD:2 * K_WIDTH + (h + 1) * HEAD]
        bcol = beta[rows, h:h + 1]
        egc = eg_col[rows, h:h + 1]
        decay = jnp.exp(jnp.where(causal, gc_col[rows, h:h + 1] - gc_row[h:h + 1, rows], -jnp.inf))
        kb = k * bcol
        lows[h, gi] = jnp.where(strict, _dot(kb.astype(BF16), kt16) * decay, 0.0)
        qk16s[h, gi] = (_dot(q.astype(BF16), kt16) * decay).astype(BF16)
        rhss[h, gi] = jnp.concatenate([v * bcol, kb * egc], axis=1)
        qgs[h, gi] = q * egc
        kdts[h, gi] = (kts[h // rep][:, rows] * kd_row[h:h + 1, rows]).astype(BF16)

    invs = {u: eye - lows[u] for u in units}
    powers = {u: _mm(lows[u], lows[u], passes) for u in units}
    span = 4
    while span < CHUNK:
        prods = {u: _mm(jnp.concatenate([powers[u], invs[u]], axis=0), powers[u], passes) for u in units}
        powers = {u: prods[u][0:GROUP] for u in units}
        invs = {u: invs[u] + prods[u][GROUP:2 * GROUP] for u in units}
        span *= 2
    invs = {u: invs[u] + _mm(invs[u], powers[u], passes) for u in units}
    uws = {u: _mm(invs[u], rhss[u], passes) for u in units}

    sts = [state_ref[h] for h in range(V_HEADS)]
    outs = [[] for _ in range(V_HEADS)]
    for gi in range(ngr):
        for c in range(GROUP // CHUNK):
            r0, r1 = c * CHUNK, (c + 1) * CHUNK
            t0 = gi * GROUP + r0
            pss = []
            for h in range(V_HEADS):
                lhs = jnp.concatenate([uws[h, gi][r0:r1, HEAD:2 * HEAD], qgs[h, gi][r0:r1]], axis=0)
                pss.append(_dot(lhs.astype(BF16), sts[h].astype(BF16)))
            rss = []
            for h in range(V_HEADS):
                vnew = uws[h, gi][r0:r1, 0:HEAD] - pss[h][0:CHUNK]
                pieces = []
                if r0 > 0:
                    pieces.append(jnp.zeros((r0, HEAD), F32))
                pieces.append(vnew)
                if r1 < GROUP:
                    pieces.append(jnp.zeros((GROUP - r1, HEAD), F32))
                vfull = jnp.concatenate(pieces, axis=0).astype(BF16)
                rss.append(_dot(jnp.concatenate([qk16s[h, gi][r0:r1], kdts[h, gi]], axis=0), vfull))
            for h in range(V_HEADS):
                outs[h].append(pss[h][CHUNK:2 * CHUNK] + rss[h][0:CHUNK])
                sts[h] = sts[h] * egl_row[h:h + 1, t0:t0 + 1] + rss[h][CHUNK:CHUNK + HEAD]

    for h in range(V_HEADS):
        state_ref[h] = sts[h]
        o = _rms(jnp.concatenate(outs[h], axis=0), ng_ref[...])
        zz = z_ref[:, h * HEAD:(h + 1) * HEAD].astype(F32)
        o_ref[:, h * HEAD:(h + 1) * HEAD] = (o * (zz * _sigmoid(zz))).astype(BF16)


def _gdn(p_main, p_ba, cw, alog, dtb, ng, *, batch, seq, ts, passes):
    t = p_main.shape[0]
    ns = seq // ts
    qkv_w = 2 * K_WIDTH + V_WIDTH
    row = lambda b, s: b * ns + s
    return pl.pallas_call(
        functools.partial(_gdn_kernel, ts=ts, passes=passes),
        grid=(batch, ns),
        in_specs=[
            pl.BlockSpec((ts, 2 * K_WIDTH), lambda b, s: (row(b, s), 2)),
            pl.BlockSpec((ts, V_WIDTH), lambda b, s: (row(b, s), 3)),
            pl.BlockSpec((ts, V_WIDTH), lambda b, s: (row(b, s), 4)),
            pl.BlockSpec((ts, BA_WIDTH), lambda b, s: (row(b, s), 0)),
            pl.BlockSpec((SHORT_K, qkv_w), lambda b, s: (0, 0)),
            pl.BlockSpec((1, LANES), lambda b, s: (0, 0)),
            pl.BlockSpec((1, LANES), lambda b, s: (0, 0)),
            pl.BlockSpec((1, HEAD), lambda b, s: (0, 0)),
        ],
        out_specs=pl.BlockSpec((ts, V_WIDTH), lambda b, s: (row(b, s), 0)),
        out_shape=jax.ShapeDtypeStruct((t, V_WIDTH), BF16),
        scratch_shapes=[
            pltpu.VMEM((SUBLANES + ts, qkv_w), F32),
            pltpu.VMEM((V_HEADS, HEAD, HEAD), F32),
        ],
        compiler_params=pltpu.CompilerParams(
            dimension_semantics=("parallel", "arbitrary"),
            vmem_limit_bytes=VMEM_LIMIT),
        name="gdn",
    )(p_main, p_main, p_main, p_ba, cw, alog, dtb, ng)


def _outproj_kernel(x_ref, oa_ref, ob_ref, wa_ref, wb_ref, o_ref):
    o_ref[...] = x_ref[...] + _dot(oa_ref[...], wa_ref[...]) + _dot(ob_ref[...], wb_ref[...])


def _outproj(x2d, oa, ob, wa, wb, *, tm):
    t, d = x2d.shape
    return pl.pallas_call(
        _outproj_kernel,
        grid=(t // tm,),
        in_specs=[
            pl.BlockSpec((tm, d), lambda i: (i, 0)),
            pl.BlockSpec((tm, CONV_CH), lambda i: (i, 0)),
            pl.BlockSpec((tm, V_WIDTH), lambda i: (i, 0)),
            pl.BlockSpec((CONV_CH, d), lambda i: (0, 0)),
            pl.BlockSpec((V_WIDTH, d), lambda i: (0, 0)),
        ],
        out_specs=pl.BlockSpec((tm, d), lambda i: (i, 0)),
        out_shape=jax.ShapeDtypeStruct((t, d), F32),
        compiler_params=pltpu.CompilerParams(
            dimension_semantics=("parallel",),
            vmem_limit_bytes=VMEM_LIMIT),
        name="outproj",
    )(x2d, oa, ob, wa, wb)


def _ffn_kernel(x_hbm, g_ref, wg_ref, wu_ref, cw_ref, cb_ref, wd_ref, fg_ref,
                o_ref, xbuf_ref, h_ref, act_ref, carry_ref, xsem, *, tm, ns, nrow, nf, final):
    s = pl.program_id(1)
    f = pl.program_id(2)
    row = pl.program_id(0) * ns + s

    def x_copy(r):
        return pltpu.make_async_copy(x_hbm.at[pl.ds(r * tm, tm), :], xbuf_ref, xsem)

    @pl.when(f == 0)
    def _():
        @pl.when(row == 0)
        def _():
            x_copy(row).start()

        x_copy(row).wait()
        x = xbuf_ref[...]
        h_ref[...] = _rms(x, g_ref[...]).astype(BF16)
        o_ref[...] = x
        act_ref[1] = jnp.zeros(act_ref.shape[1:], BF16)

        @pl.when(s == 0)
        def _():
            carry_ref[...] = jnp.zeros(carry_ref.shape, F32)

    @pl.when(jnp.logical_and(f == 1, row + 1 < nrow))
    def _():
        x_copy(row + 1).start()

    def step(slot):
        h = h_ref[...]
        gate = _dot(h, wg_ref[...])
        up = _dot(h, wu_ref[...])
        o_ref[...] += _dot(act_ref[1 - slot], wd_ref[...])
        g_all = jnp.concatenate([carry_ref[f], gate], axis=0)
        carry_ref[f] = gate[tm - SUBLANES:tm, :]
        gc = cw_ref[FFN_K - 1:FFN_K, :] * gate + cb_ref[...]
        for j in range(FFN_K - 1):
            delayed = pltpu.roll(g_all, FFN_K - 1 - j, axis=0)
            gc = gc + cw_ref[j:j + 1, :] * delayed[SUBLANES:SUBLANES + tm]
        act_ref[slot] = (gc * _sigmoid(gc) * up).astype(BF16)

    for slot in range(2):
        pl.when(jnp.logical_and(f < nf, f % 2 == slot))(functools.partial(step, slot))

    @pl.when(f == nf)
    def _():
        out = o_ref[...] + _dot(act_ref[(nf - 1) % 2], wd_ref[...])
        if final:
            out = _rms(out, fg_ref[...])
        o_ref[...] = out


def _ffn(x2d, g, w_up, cw, cb, w_down, fg, *, batch, seq, tm, tf, final):
    t, d = x2d.shape
    dff = w_down.shape[0]
    ns = seq // tm
    nf = dff // tf
    cur = lambda f: jnp.minimum(f, nf - 1)
    prev = lambda f: jnp.maximum(f - 1, 0)
    return pl.pallas_call(
        functools.partial(_ffn_kernel, tm=tm, ns=ns, nrow=batch * ns, nf=nf, final=final),
        grid=(batch, ns, nf + 1),
        in_specs=[
            pl.BlockSpec(memory_space=pl.ANY),
            pl.BlockSpec((1, d), lambda b, s, f: (0, 0)),
            pl.BlockSpec((d, tf), lambda b, s, f: (0, cur(f))),
            pl.BlockSpec((d, tf), lambda b, s, f: (0, nf + cur(f))),
            pl.BlockSpec((FFN_K, tf), lambda b, s, f: (0, cur(f))),
            pl.BlockSpec((1, tf), lambda b, s, f: (0, cur(f))),
            pl.BlockSpec((tf, d), lambda b, s, f: (prev(f), 0)),
            pl.BlockSpec((1, d), lambda b, s, f: (0, 0)),
        ],
        out_specs=pl.BlockSpec((tm, d), lambda b, s, f: (b * ns + s, 0)),
        out_shape=jax.ShapeDtypeStruct((t, d), F32),
        scratch_shapes=[
            pltpu.VMEM((tm, d), F32),
            pltpu.VMEM((tm, d), BF16),
            pltpu.VMEM((2, tm, tf), BF16),
            pltpu.VMEM((nf, SUBLANES, tf), F32),
            pltpu.SemaphoreType.DMA(()),
        ],
        compiler_params=pltpu.CompilerParams(
            dimension_semantics=("arbitrary", "arbitrary", "arbitrary"),
            vmem_limit_bytes=VMEM_LIMIT),
        name="ffn",
    )(x2d, g, w_up, w_up, cw, cb, w_down, fg)


def _pad_lanes(vec):
    return jnp.zeros((1, LANES), F32).at[0, 0:vec.shape[0]].set(vec)


def kernel(x, mix_norm_g, w_in, conv_dw_w, conv_dw_b, conv_ln_g, conv_ln_b, conv_pw_w, conv_pw_b,
           gdn_conv_w, gdn_a_log, gdn_dt_bias, gdn_norm_g, w_out, ffn_norm_g, w_up,
           ffn_conv_w, ffn_conv_b, w_down, final_norm_g):
    batch, seq, d = x.shape
    depth = w_in.shape[0]
    t = batch * seq
    x2d = x.reshape(t, d)
    fg = final_norm_g.reshape(1, d)
    b_off = MAIN_WIDTH
    a_off = MAIN_WIDTH + V_HEADS

    for l in range(depth):
        w_main = w_in[l, :, 0:MAIN_WIDTH].astype(BF16)
        w_ba = jnp.zeros((d, BA_WIDTH), F32)
        w_ba = w_ba.at[:, 0:V_HEADS].set(w_in[l, :, b_off:b_off + V_HEADS])
        w_ba = w_ba.at[:, LANES:LANES + V_HEADS].set(w_in[l, :, a_off:a_off + V_HEADS])
        p_main, p_ba = _inproj(x2d, mix_norm_g[l].reshape(1, d), w_main, w_ba.astype(BF16),
                               tm=min(1024, seq), tn=1280)
        out_a = _conformer(p_main, conv_dw_w[l], conv_dw_b[l].reshape(1, -1),
                           conv_ln_g[l].reshape(1, -1), conv_ln_b[l].reshape(1, -1),
                           conv_pw_w[l].astype(BF16), conv_pw_b[l].reshape(1, -1),
                           batch=batch, seq=seq, ts=min(1024, seq))
        out_b = _gdn(p_main, p_ba, gdn_conv_w[l], _pad_lanes(gdn_a_log[l]),
                     _pad_lanes(gdn_dt_bias[l]), gdn_norm_g[l].reshape(1, -1),
                     batch=batch, seq=seq, ts=min(512, seq), passes=1)
        wo = w_out[l].astype(BF16)
        x2d = _outproj(x2d, out_a, out_b, wo[0:CONV_CH], wo[CONV_CH:], tm=min(512, seq))
        x2d = _ffn(x2d, ffn_norm_g[l].reshape(1, d), w_up[l].astype(BF16), ffn_conv_w[l],
                   ffn_conv_b[l].reshape(1, -1), w_down[l].astype(BF16), fg,
                   batch=batch, seq=seq, tm=min(1024, seq), tf=512, final=(l == depth - 1))
    return x2d.reshape(batch, seq, d)
```
